```python
import math
import jax, jax.numpy as jnp
from jax import lax
import numpy as np

D_MODEL = 2048
BATCH = 16
SEQ = 2048
DEPTH = 1
DEC_BATCH = 8
DEC_SEQ = 4096
PAST_LEN = 128

N_HEADS = 8
HEAD_DIM = 64
ATTN_QK = N_HEADS * 2 * HEAD_DIM
ATTN_V = N_HEADS * 2 * HEAD_DIM
ROT_DIM = HEAD_DIM // 4
ROPE_THETA = 500000.0
Q_BLOCK = 128
SSM_GROUP = 16
SSM_GROUPS = 64
D_SSM = SSM_GROUP * SSM_GROUPS
SSM_STATE = 64
N_IN = 2 * ATTN_QK + ATTN_V + D_SSM + 2 * D_MODEL
SPLITS = [ATTN_QK, 2 * ATTN_QK, 2 * ATTN_QK + ATTN_V, 2 * ATTN_QK + ATTN_V + D_SSM, 2 * ATTN_QK + ATTN_V + D_SSM + D_MODEL]
N_EXPERTS = 32
TOP_K = 4
D_FF = D_MODEL
SWIGLU_LIMIT = 7.0
SWIGLU_ALPHA = 1.702
MOE_BLOCK = 256
DN_ALPHA = (2.0 * DEPTH) ** 0.25
DN_BETA = (8.0 * DEPTH) ** -0.25
LN_EPS = 1e-5

kernel_name = 'hybrid_s5_diffattn_moe_encoder'

F32 = jnp.float32


def layer_norm(x, g, b):
    xf = x.astype(F32)
    mu = jnp.mean(xf, axis=-1, keepdims=True)
    var = jnp.mean(jnp.square(xf - mu), axis=-1, keepdims=True)
    return ((xf - mu) * lax.rsqrt(var + LN_EPS) * g.astype(F32) + b.astype(F32)).astype(x.dtype)


def rope_tables(s):
    inv = ROPE_THETA ** (-jnp.arange(0, ROT_DIM, 2, dtype=F32) / ROT_DIM)
    ang = jnp.arange(s, dtype=F32)[:, None] * inv[None, :]
    return jnp.cos(ang), jnp.sin(ang)


def apply_partial_rope(x, cos, sin):
    half = ROT_DIM // 2
    xr = x[..., :ROT_DIM].astype(F32)
    x1, x2 = xr[..., :half], xr[..., half:]
    c = cos[None, :, None, None, :]
    sn = sin[None, :, None, None, :]
    rot = jnp.concatenate([x1 * c - x2 * sn, x2 * c + x1 * sn], axis=-1).astype(x.dtype)
    return jnp.concatenate([rot, x[..., ROT_DIM:]], axis=-1)


def diff_attention(q, k, v, lam, subln_g, lambda_init):
    b, s = q.shape[0], q.shape[1]
    nq = s // Q_BLOCK
    kf = k.astype(F32)
    vf = v.astype(F32)
    qb = (q.astype(F32) * (HEAD_DIM ** -0.5)).reshape(b, nq, Q_BLOCK, N_HEADS, 2, HEAD_DIM)
    qb = jnp.moveaxis(qb, 1, 0)

    def block(qi):
        sc = jnp.einsum('bqhcd,bkhcd->bhcqk', qi, kf)
        p = jax.nn.softmax(sc, axis=-1)
        w = p[:, :, 0] - lam * p[:, :, 1]
        return jnp.einsum('bhqk,bkhe->bqhe', w, vf)

    o = lax.map(block, qb)
    o = jnp.moveaxis(o, 0, 1).reshape(b, s, N_HEADS, 2 * HEAD_DIM)
    o = o * lax.rsqrt(jnp.mean(jnp.square(o), axis=-1, keepdims=True) + LN_EPS) * subln_g.astype(F32)
    o = o * (1.0 - lambda_init)
    return o.reshape(b, s, ATTN_V).astype(q.dtype)


def s5_discretize(lam_re, lam_im, log_dt, b_re, b_im):
    lr = jnp.minimum(lam_re.astype(F32), -1e-4)
    li = lam_im.astype(F32)
    dt = jnp.exp(log_dt.astype(F32))[..., None]
    mag = jnp.exp(lr * dt)
    abr = mag * jnp.cos(li * dt)
    abi = mag * jnp.sin(li * dt)
    nr = abr - 1.0
    den = lr * lr + li * li
    cr = ((nr * lr + abi * li) / den)[..., None]
    ci = ((abi * lr - nr * li) / den)[..., None]
    br = b_re.astype(F32)
    bi = b_im.astype(F32)
    return abr, abi, cr * br - ci * bi, cr * bi + ci * br


def _cplx_combine(e1, e2):
    a1r, a1i, b1r, b1i = e1
    a2r, a2i, b2r, b2i = e2
    return (a2r * a1r - a2i * a1i,
            a2r * a1i + a2i * a1r,
            a2r * b1r - a2i * b1i + b2r,
            a2r * b1i + a2i * b1r + b2i)


def s5_scan_direction(us, abr, abi, bbr, bbi, cr, ci, reverse):
    bu_r = jnp.einsum('sgc,gpc->sgp', us, bbr)
    bu_i = jnp.einsum('sgc,gpc->sgp', us, bbi)
    a_r = jnp.broadcast_to(abr, bu_r.shape)
    a_i = jnp.broadcast_to(abi, bu_r.shape)
    _, _, xr, xi = lax.associative_scan(_cplx_combine, (a_r, a_i, bu_r, bu_i), reverse=reverse, axis=0)
    return jnp.einsum('sgp,gcp->sgc', xr, cr) - jnp.einsum('sgp,gcp->sgc', xi, ci)


def s5_branch(u, lam_re, lam_im, log_dt, b_re, b_im, c_re, c_im, d_skip, w_glu, b_glu):
    b, s, _ = u.shape
    abr, abi, bbr, bbi = s5_discretize(lam_re, lam_im, log_dt, b_re, b_im)
    cr = c_re.astype(F32)
    ci = c_im.astype(F32)
    uf = u.astype(F32)
    ug = uf.reshape(b, s, SSM_GROUPS, SSM_GROUP)

    def one_seq(us):
        y_fwd = s5_scan_direction(us, abr[0], abi[0], bbr[0], bbi[0], cr[0], ci[0], False)
        y_bwd = s5_scan_direction(us, abr[1], abi[1], bbr[1], bbi[1], cr[1], ci[1], True)
        return y_fwd + y_bwd

    y = lax.map(one_seq, ug).reshape(b, s, D_SSM) + d_skip.astype(F32) * uf
    y = jax.nn.gelu(y)
    y = y * jax.nn.sigmoid(y @ w_glu.astype(F32) + b_glu.astype(F32))
    return y.astype(u.dtype)


def moe_ffn(t, w_router, b_router, w_gate, b_gate, w_up, b_up, w_down, b_down):
    n, d = t.shape
    logits = t.astype(F32) @ w_router.astype(F32) + b_router.astype(F32)
    top_v, top_e = lax.top_k(logits, TOP_K)
    gates = jax.nn.softmax(top_v, axis=-1)
    e_flat = top_e.reshape(-1).astype(jnp.int32)
    g_flat = gates.reshape(-1)
    tok_flat = jnp.arange(n * TOP_K, dtype=jnp.int32) // TOP_K
    order = jnp.argsort(e_flat)
    e_sorted = e_flat[order]
    counts = jnp.bincount(e_flat, length=N_EXPERTS)
    start = jnp.cumsum(counts) - counts
    padded = (counts + MOE_BLOCK - 1) // MOE_BLOCK * MOE_BLOCK
    pend = jnp.cumsum(padded)
    pstart = pend - padded
    rank = jnp.arange(n * TOP_K, dtype=jnp.int32) - start[e_sorted]
    dest = pstart[e_sorted] + rank
    n_rows = n * TOP_K + N_EXPERTS * MOE_BLOCK
    n_blocks = n_rows // MOE_BLOCK
    row_tok = jnp.full((n_rows,), n, dtype=jnp.int32).at[dest].set(tok_flat[order])
    row_gate = jnp.zeros((n_rows,), F32).at[dest].set(g_flat[order])
    block_e = jnp.clip(jnp.searchsorted(pend, jnp.arange(n_blocks, dtype=jnp.int32) * MOE_BLOCK, side='right'), 0, N_EXPERTS - 1)
    t_pad = jnp.concatenate([t, jnp.zeros((1, d), t.dtype)], axis=0)
    xb = t_pad[row_tok].reshape(n_blocks, MOE_BLOCK, d)

    def expert_block(args):
        xe, e = args
        g = xe @ w_gate[e] + b_gate[e]
        u = xe @ w_up[e] + b_up[e]
        g = jnp.minimum(g, SWIGLU_LIMIT)
        u = jnp.clip(u, -SWIGLU_LIMIT, SWIGLU_LIMIT)
        h = g * jax.nn.sigmoid(SWIGLU_ALPHA * g) * (u + 1.0)
        return h @ w_down[e] + b_down[e]

    yb = lax.map(expert_block, (xb, block_e)).reshape(n_rows, d)
    y = jnp.zeros((n + 1, d), F32).at[row_tok].add(yb.astype(F32) * row_gate[:, None])
    return y[:n].astype(t.dtype)


def encoder_layer(x, l, cos, sin, w_in, lam_q1, lam_k1, lam_q2, lam_k2, subln_g, w_attn_proj,
                  ssm_lam_re, ssm_lam_im, ssm_log_dt, ssm_b_re, ssm_b_im, ssm_c_re, ssm_c_im, ssm_d,
                  w_glu, b_glu, w_ssm_proj, w_out, ln1_g, ln1_b, w_router, b_router,
                  w_gate, b_gate, w_up, b_up, w_down, b_down, ln2_g, ln2_b):
    b, s, d = x.shape
    proj = x @ w_in
    q, k, v, u, g_ssm, g_attn = jnp.split(proj, SPLITS, axis=-1)
    q = apply_partial_rope(q.reshape(b, s, N_HEADS, 2, HEAD_DIM), cos, sin)
    k = apply_partial_rope(k.reshape(b, s, N_HEADS, 2, HEAD_DIM), cos, sin)
    v = v.reshape(b, s, N_HEADS, 2 * HEAD_DIM)
    lambda_init = 0.8 - 0.6 * math.exp(-0.3 * l)
    lam = (jnp.exp(jnp.sum(lam_q1.astype(F32) * lam_k1.astype(F32)))
           - jnp.exp(jnp.sum(lam_q2.astype(F32) * lam_k2.astype(F32))) + lambda_init)
    attn = diff_attention(q, k, v, lam, subln_g, lambda_init)
    ssm = s5_branch(u, ssm_lam_re, ssm_lam_im, ssm_log_dt, ssm_b_re, ssm_b_im,
                    ssm_c_re, ssm_c_im, ssm_d, w_glu, b_glu)
    merged = jax.nn.sigmoid(g_attn) * (attn @ w_attn_proj) + jax.nn.sigmoid(g_ssm) * (ssm @ w_ssm_proj)
    x = layer_norm(DN_ALPHA * x + merged @ w_out, ln1_g, ln1_b)
    ff = moe_ffn(x.reshape(b * s, d), w_router, b_router, w_gate, b_gate, w_up, b_up, w_down, b_down)
    return layer_norm(DN_ALPHA * x + ff.reshape(b, s, d), ln2_g, ln2_b)


def encoder_trunk(x, ln_in_g, ln_in_b, w_in, lam_q1, lam_k1, lam_q2, lam_k2, subln_g, w_attn_proj,
                  ssm_lam_re, ssm_lam_im, ssm_log_dt, ssm_b_re, ssm_b_im, ssm_c_re, ssm_c_im, ssm_d,
                  w_glu, b_glu, w_ssm_proj, w_out, ln1_g, ln1_b, w_router, b_router,
                  w_gate, b_gate, w_up, b_up, w_down, b_down, ln2_g, ln2_b):
    cos, sin = rope_tables(x.shape[1])
    x = layer_norm(x, ln_in_g, ln_in_b)
    for l in range(DEPTH):
        x = encoder_layer(x, l, cos, sin, w_in[l], lam_q1[l], lam_k1[l], lam_q2[l], lam_k2[l], subln_g[l],
                          w_attn_proj[l], ssm_lam_re[l], ssm_lam_im[l], ssm_log_dt[l], ssm_b_re[l], ssm_b_im[l],
                          ssm_c_re[l], ssm_c_im[l], ssm_d[l], w_glu[l], b_glu[l], w_ssm_proj[l], w_out[l],
                          ln1_g[l], ln1_b[l], w_router[l], b_router[l], w_gate[l], b_gate[l], w_up[l], b_up[l],
                          w_down[l], b_down[l], ln2_g[l], ln2_b[l])
    return x


def setup_inputs(seed: int = 0) -> dict:
    key = jax.random.key(seed)
    ks = jax.random.split(key, 40)

    def nrm(i, shape, scale):
        return jax.random.normal(ks[i], shape, F32) * scale

    G, P, CG, E = SSM_GROUPS, SSM_STATE, SSM_GROUP, N_EXPERTS
    col_scale = jnp.ones((N_IN,), F32).at[2 * ATTN_QK:2 * ATTN_QK + ATTN_V].set(DN_BETA)
    lam_im_base = jnp.pi * jnp.arange(P, dtype=F32)
    return {
        'x_prompt': nrm(0, (BATCH, SEQ, D_MODEL), 1.0),
        'x_sample': nrm(1, (DEC_BATCH, DEC_SEQ, D_MODEL), 1.0),
        'ln_in_g': 1.0 + nrm(2, (D_MODEL,), 0.02),
        'ln_in_b': nrm(3, (D_MODEL,), 0.02),
        'w_in': nrm(4, (DEPTH, D_MODEL, N_IN), D_MODEL ** -0.5) * col_scale,
        'lam_q1': nrm(5, (DEPTH, HEAD_DIM), 0.1),
        'lam_k1': nrm(6, (DEPTH, HEAD_DIM), 0.1),
        'lam_q2': nrm(7, (DEPTH, HEAD_DIM), 0.1),
        'lam_k2': nrm(8, (DEPTH, HEAD_DIM), 0.1),
        'subln_g': 1.0 + nrm(9, (DEPTH, 2 * HEAD_DIM), 0.02),
        'w_attn_proj': nrm(10, (DEPTH, ATTN_V, D_MODEL), ATTN_V ** -0.5 * DN_BETA),
        'ssm_lam_re': -0.5 + nrm(11, (DEPTH, 2, G, P), 0.01),
        'ssm_lam_im': lam_im_base + nrm(12, (DEPTH, 2, G, P), 0.01),
        'ssm_log_dt': jax.random.uniform(ks[13], (DEPTH, 2, G), F32, math.log(1e-3), math.log(1e-1)),
        'ssm_b_re': nrm(14, (DEPTH, 2, G, P, CG), (2 * CG) ** -0.5),
        'ssm_b_im': nrm(15, (DEPTH, 2, G, P, CG), (2 * CG) ** -0.5),
        'ssm_c_re': nrm(16, (DEPTH, 2, G, CG, P), P ** -0.5),
        'ssm_c_im': nrm(17, (DEPTH, 2, G, CG, P), P ** -0.5),
        'ssm_d': nrm(18, (DEPTH, D_SSM), 1.0),
        'w_glu': nrm(19, (DEPTH, D_SSM, D_SSM), D_SSM ** -0.5),
        'b_glu': nrm(20, (DEPTH, D_SSM), 0.02),
        'w_ssm_proj': nrm(21, (DEPTH, D_SSM, D_MODEL), D_SSM ** -0.5 * DN_BETA),
        'w_out': nrm(22, (DEPTH, D_MODEL, D_MODEL), D_MODEL ** -0.5 * DN_BETA),
        'ln1_g': 1.0 + nrm(23, (DEPTH, D_MODEL), 0.02),
        'ln1_b': nrm(24, (DEPTH, D_MODEL), 0.02),
        'w_router': nrm(25, (DEPTH, D_MODEL, E), D_MODEL ** -0.5),
        'b_router': nrm(26, (DEPTH, E), 0.01),
        'w_gate': nrm(27, (DEPTH, E, D_MODEL, D_FF), D_MODEL ** -0.5 * DN_BETA),
        'b_gate': nrm(28, (DEPTH, E, D_FF), 0.02),
        'w_up': nrm(29, (DEPTH, E, D_MODEL, D_FF), D_MODEL ** -0.5 * DN_BETA),
        'b_up': nrm(30, (DEPTH, E, D_FF), 0.02),
        'w_down': nrm(31, (DEPTH, E, D_FF, D_MODEL), D_FF ** -0.5 * DN_BETA),
        'b_down': nrm(32, (DEPTH, E, D_MODEL), 0.02),
        'ln2_g': 1.0 + nrm(33, (DEPTH, D_MODEL), 0.02),
        'ln2_b': nrm(34, (DEPTH, D_MODEL), 0.02),
    }


def reference(x_prompt, x_sample, ln_in_g, ln_in_b, w_in, lam_q1, lam_k1, lam_q2, lam_k2, subln_g, w_attn_proj,
              ssm_lam_re, ssm_lam_im, ssm_log_dt, ssm_b_re, ssm_b_im, ssm_c_re, ssm_c_im, ssm_d,
              w_glu, b_glu, w_ssm_proj, w_out, ln1_g, ln1_b, w_router, b_router,
              w_gate, b_gate, w_up, b_up, w_down, b_down, ln2_g, ln2_b):
    params = (ln_in_g, ln_in_b, w_in, lam_q1, lam_k1, lam_q2, lam_k2, subln_g, w_attn_proj,
              ssm_lam_re, ssm_lam_im, ssm_log_dt, ssm_b_re, ssm_b_im, ssm_c_re, ssm_c_im, ssm_d,
              w_glu, b_glu, w_ssm_proj, w_out, ln1_g, ln1_b, w_router, b_router,
              w_gate, b_gate, w_up, b_up, w_down, b_down, ln2_g, ln2_b)
    y_prompt = encoder_trunk(x_prompt, *params)
    y_sample = encoder_trunk(x_sample, *params)
    return (y_prompt, y_sample)
```

```python
import functools
import math

import jax
import jax.numpy as jnp
from jax import lax
from jax.experimental import pallas as pl
from jax.experimental.pallas import tpu as pltpu

F32 = jnp.float32
BF16 = jnp.bfloat16
U32 = jnp.uint32
I32 = jnp.int32

D_MODEL = 2048
DEPTH = 1
N_HEADS = 8
HEAD_DIM = 64
HEAD_W = 2 * HEAD_DIM
ATTN_W = N_HEADS * HEAD_W
ROT_DIM = HEAD_DIM // 4
ROPE_THETA = 500000.0
SSM_GROUP = 16
SSM_GROUPS = 64
D_SSM = SSM_GROUP * SSM_GROUPS
SSM_STATE = 64
S5_CHUNK = 16
S5_W = S5_CHUNK * SSM_GROUP
N_IN = 3 * ATTN_W + D_SSM + 2 * D_MODEL
N_EXPERTS = 32
TOP_K = 4
D_FF = D_MODEL
SWIGLU_LIMIT = 7.0
SWIGLU_ALPHA = 1.702
DN_ALPHA = (2.0 * DEPTH) ** 0.25
LN_EPS = 1e-5
HALF = D_MODEL // 2

VMEM_LIMIT = 56 * 1024 * 1024


def _cparams(sem, **kw):
    return pltpu.CompilerParams(dimension_semantics=sem, vmem_limit_bytes=VMEM_LIMIT, **kw)


def _ln(x, g, b):
    mu = jnp.mean(x, axis=-1, keepdims=True)
    xc = x - mu
    var = jnp.mean(xc * xc, axis=-1, keepdims=True)
    return xc * lax.rsqrt(var + LN_EPS) * g + b


def _sigmoid(x):
    return 1.0 / (1.0 + jnp.exp(-x))


def _gelu_tanh(x):
    return 0.5 * x * (1.0 + jnp.tanh(math.sqrt(2.0 / math.pi) * (x + 0.044715 * (x * x * x))))


def _pack_rows(x):
    lo = lax.bitcast_convert_type(x[:, :HALF].astype(BF16).astype(F32), U32)
    hi = lax.bitcast_convert_type(x[:, HALF:].astype(BF16).astype(F32), U32)
    return (hi & jnp.uint32(0xFFFF0000)) | (lo >> 16)


def _unpack_rows(w):
    lo = lax.bitcast_convert_type(w << 16, F32)
    hi = lax.bitcast_convert_type(w & jnp.uint32(0xFFFF0000), F32)
    return lo, hi


ROW_TILE = 8
LANES = 128


def _store_row_tiles(ref, r0, nrows, packed):
    for s in range(ROW_TILE):
        ref[pl.ds(r0 * ROW_TILE + s, nrows, stride=ROW_TILE), :] = packed[:, s * LANES:(s + 1) * LANES]


def _load_row_tile_words(ref, r0, nrows, s):
    return ref[pl.ds(r0 * ROW_TILE + s, nrows, stride=ROW_TILE), :]


def _inproj_kernel(x_ref, g_ref, b_ref, w_ref, o_ref, xn_ref, *, tm, rc):
    @pl.when(pl.program_id(1) == 0)
    def _():
        def body(c, carry):
            r = pl.ds(pl.multiple_of(c * rc, rc), rc)
            xn_ref[r, :] = _ln(x_ref[r, :], g_ref[...], b_ref[...]).astype(BF16)
            return carry
        lax.fori_loop(0, tm // rc, body, 0)

    o_ref[...] = jnp.dot(xn_ref[...], w_ref[...], preferred_element_type=F32).astype(BF16)


def _inproj(x, g, b, w_bf16):
    n = x.shape[0]
    tm = min(1024, n)
    tn = 512
    rc = min(256, tm)
    return pl.pallas_call(
        functools.partial(_inproj_kernel, tm=tm, rc=rc),
        grid=(n // tm, N_IN // tn),
        in_specs=[
            pl.BlockSpec((tm, D_MODEL), lambda i, j: (i, 0)),
            pl.BlockSpec((1, D_MODEL), lambda i, j: (0, 0)),
            pl.BlockSpec((1, D_MODEL), lambda i, j: (0, 0)),
            pl.BlockSpec((D_MODEL, tn), lambda i, j: (0, j)),
        ],
        out_specs=pl.BlockSpec((tm, tn), lambda i, j: (i, j)),
        out_shape=jax.ShapeDtypeStruct((n, N_IN), BF16),
        scratch_shapes=[pltpu.VMEM((tm, D_MODEL), BF16)],
        compiler_params=_cparams(("parallel", "arbitrary")),
        name="ln_inproj",
    )(x, g, b, w_bf16)


def _rope(x, c, s):
    lane = lax.broadcasted_iota(I32, (1, HEAD_W), 1)
    first_half = (lane % HEAD_DIM) < (ROT_DIM // 2)
    xs = jnp.where(first_half, pltpu.roll(x, HEAD_W - ROT_DIM // 2, 1), pltpu.roll(x, ROT_DIM // 2, 1))
    return x * c + xs * s


def _attn_kernel(lam_ref, q_ref, k_ref, v_ref, cq_ref, sq_ref, ck_ref, sk_ref, g_ref, o_ref,
                 k_scr, q_scr, m_scr, l_scr, acc_scr, *, seq, tq, tk, rc, out_scale):
    @pl.when(pl.program_id(2) == 0)
    def _():
        def body(c, carry):
            r = pl.ds(pl.multiple_of(c * rc, rc), rc)
            k_scr[r, :] = _rope(k_ref[r, :].astype(F32), ck_ref[r, :], sk_ref[r, :]).astype(BF16)
            return carry
        lax.fori_loop(0, seq // rc, body, 0)

    q = _rope(q_ref[...].astype(F32), cq_ref[...], sq_ref[...]) * (HEAD_DIM ** -0.5)
    lane = lax.broadcasted_iota(I32, (1, HEAD_W), 1)
    comp1 = lane < HEAD_DIM
    q_scr[pl.ds(0, tq), :] = jnp.where(comp1, q, 0.0).astype(BF16)
    q_scr[pl.ds(tq, tq), :] = jnp.where(comp1, 0.0, q).astype(BF16)
    m_scr[...] = jnp.full(m_scr.shape, -jnp.inf, F32)
    l_scr[...] = jnp.zeros(l_scr.shape, F32)
    acc_scr[...] = jnp.zeros(acc_scr.shape, F32)

    def body(j, carry):
        r = pl.ds(pl.multiple_of(j * tk, tk), tk)
        s = lax.dot_general(q_scr[...], k_scr[r, :], (((1,), (1,)), ((), ())),
                            preferred_element_type=F32)
        m_old = m_scr[...]
        m_new = jnp.maximum(m_old, jnp.max(s, axis=-1, keepdims=True))
        p = jnp.exp(s - m_new)
        a = jnp.exp(m_old - m_new)
        l_scr[...] = a * l_scr[...] + jnp.sum(p, axis=-1, keepdims=True)
        acc_scr[...] = a * acc_scr[...] + jnp.dot(p.astype(BF16), v_ref[r, :], preferred_element_type=F32)
        m_scr[...] = m_new
        return carry
    lax.fori_loop(0, seq // tk, body, 0)

    o = acc_scr[...] / l_scr[...]
    o = o[:tq] - lam_ref[0, 0] * o[tq:]
    o = o * lax.rsqrt(jnp.mean(o * o, axis=-1, keepdims=True) + LN_EPS) * g_ref[...] * out_scale
    o_ref[...] = o.astype(BF16)


def _attention(proj, lam, cos_t, sin_t, subln_g, batch, seq, lambda_init):
    tq = min(256, seq)
    tk = min(512, seq)
    rc = min(512, seq)
    nq = seq // tq
    kern = functools.partial(_attn_kernel, seq=seq, tq=tq, tk=tk, rc=rc, out_scale=1.0 - lambda_init)
    return pl.pallas_call(
        kern,
        grid=(batch, N_HEADS, nq),
        in_specs=[
            pl.BlockSpec(memory_space=pltpu.SMEM),
            pl.BlockSpec((tq, HEAD_W), lambda b, h, i: (b * nq + i, h)),
            pl.BlockSpec((seq, HEAD_W), lambda b, h, i: (b, N_HEADS + h)),
            pl.BlockSpec((seq, HEAD_W), lambda b, h, i: (b, 2 * N_HEADS + h)),
            pl.BlockSpec((tq, HEAD_W), lambda b, h, i: (i, 0)),
            pl.BlockSpec((tq, HEAD_W), lambda b, h, i: (i, 0)),
            pl.BlockSpec((seq, HEAD_W), lambda b, h, i: (0, 0)),
            pl.BlockSpec((seq, HEAD_W), lambda b, h, i: (0, 0)),
            pl.BlockSpec((1, HEAD_W), lambda b, h, i: (0, 0)),
        ],
        out_specs=pl.BlockSpec((tq, HEAD_W), lambda b, h, i: (b * nq + i, h)),
        out_shape=jax.ShapeDtypeStruct((batch * seq, ATTN_W), BF16),
        scratch_shapes=[
            pltpu.VMEM((seq, HEAD_W), BF16),
            pltpu.VMEM((2 * tq, HEAD_W), BF16),
            pltpu.VMEM((2 * tq, 1), F32),
            pltpu.VMEM((2 * tq, 1), F32),
            pltpu.VMEM((2 * tq, HEAD_W), F32),
        ],
        compiler_params=_cparams(("parallel", "parallel", "arbitrary")),
        name="diff_attention",
    )(lam, proj, proj, proj, cos_t, sin_t, cos_t, sin_t, subln_g)


def _rope_tables(seq):
    half = ROT_DIM // 2
    inv = ROPE_THETA ** (-jnp.arange(0, ROT_DIM, 2, dtype=F32) / ROT_DIM)
    ang = jnp.arange(seq, dtype=F32)[:, None] * inv[None, :]
    lane = jnp.arange(HEAD_W) % HEAD_DIM
    cos_l = jnp.cos(ang)[:, lane % half]
    sin_l = jnp.sin(ang)[:, lane % half]
    cos_t = jnp.where(lane[None, :] < ROT_DIM, cos_l, 1.0)
    sin_t = jnp.where(lane[None, :] < half, -sin_l, jnp.where(lane[None, :] < ROT_DIM, sin_l, 0.0))
    return cos_t.astype(F32), sin_t.astype(F32)


def _s5_operators(lam_re, lam_im, log_dt, b_re, b_im, c_re, c_im):
    L, G, P, CG = S5_CHUNK, SSM_GROUPS, SSM_STATE, SSM_GROUP
    hp = lax.Precision.HIGHEST
    lr = jnp.minimum(lam_re.astype(F32), -1e-4)
    li = lam_im.astype(F32)
    dt = jnp.exp(log_dt.astype(F32))[..., None]
    nn = jnp.arange(L + 1, dtype=F32)[:, None, None, None]
    mag = jnp.exp(nn * (lr * dt)[None])
    pw_r = mag * jnp.cos(nn * (li * dt)[None])
    pw_i = mag * jnp.sin(nn * (li * dt)[None])
    ab_r, ab_i = pw_r[1], pw_i[1]
    nr = ab_r - 1.0
    den = lr * lr + li * li
    cr_ = ((nr * lr + ab_i * li) / den)[..., None]
    ci_ = ((ab_i * lr - nr * li) / den)[..., None]
    br = b_re.astype(F32)
    bi = b_im.astype(F32)
    bb_r = cr_ * br - ci_ * bi
    bb_i = cr_ * bi + ci_ * br
    cr = c_re.astype(F32)
    ci = c_im.astype(F32)

    m_r = pw_r[..., None] * bb_r[None] - pw_i[..., None] * bb_i[None]
    m_i = pw_r[..., None] * bb_i[None] + pw_i[..., None] * bb_r[None]
    kk = (jnp.einsum('dgop,ndgpi->ndgoi', cr, m_r, precision=hp)
          - jnp.einsum('dgop,ndgpi->ndgoi', ci, m_i, precision=hp))
    ii = jnp.arange(L)[:, None]
    jj = jnp.arange(L)[None, :]
    kf = jnp.where((jj >= ii)[:, :, None, None, None], kk[:, 0][jnp.clip(jj - ii, 0, L)], 0.0)
    kb = jnp.where((ii >= jj)[:, :, None, None, None], kk[:, 1][jnp.clip(ii - jj, 0, L)], 0.0)
    t_op = jnp.transpose(kf + kb, (2, 0, 4, 1, 3)).reshape(G, S5_W, S5_W)

    def state_in(mr, mi):
        e_re = jnp.transpose(mr, (1, 0, 3, 2)).reshape(G, S5_W, P)
        e_im = jnp.transpose(mi, (1, 0, 3, 2)).reshape(G, S5_W, P)
        return jnp.concatenate([e_re, e_im, e_im, e_re], axis=-1)
    bm = jnp.concatenate([state_in(m_r[L - 1::-1, 0][:L], m_i[L - 1::-1, 0][:L]),
                          state_in(m_r[:L, 1], m_i[:L, 1])], axis=-1)

    def state_out(d, pows):
        pr = pw_r[pows, d]
        pi = pw_i[pows, d]
        g_r = cr[d][None] * pr[:, :, None, :] - ci[d][None] * pi[:, :, None, :]
        g_i = cr[d][None] * pi[:, :, None, :] + ci[d][None] * pr[:, :, None, :]
        rows_re = jnp.transpose(g_r, (1, 3, 0, 2)).reshape(G, P, S5_W)
        rows_im = jnp.transpose(-g_i, (1, 3, 0, 2)).reshape(G, P, S5_W)
        return jnp.concatenate([rows_re, rows_im], axis=1)
    cm = jnp.concatenate([state_out(0, jnp.arange(1, L + 1)), state_out(1, jnp.arange(L, 0, -1))], axis=1)

    def step_mult(d):
        ar = jnp.concatenate([pw_r[L, d]] * 4, axis=-1)
        ai = jnp.concatenate([-pw_i[L, d], pw_i[L, d], pw_i[L, d], -pw_i[L, d]], axis=-1)
        return ar, ai
    arf, aif = step_mult(0)
    arb, aib = step_mult(1)
    a_r = jnp.concatenate([arf, arb], axis=-1)[:, None, :]
    a_i = jnp.concatenate([aif, aib], axis=-1)[:, None, :]
    return t_op.astype(BF16), bm.astype(BF16), cm.astype(BF16), a_r, a_i


def _s5_kernel(u_ref, t_ref, bm_ref, cm_ref, ar_ref, ai_ref, d_ref, o_ref, e_scr, s_scr, *, nc, nb):
    u = u_ref[0]
    e_scr[...] = jnp.dot(u, bm_ref[0], preferred_element_type=F32)
    w2 = 2 * S5_W // 2
    arf = ar_ref[0, :, :w2]
    aif = ai_ref[0, :, :w2]
    arb = ar_ref[0, :, w2:]
    aib = ai_ref[0, :, w2:]
    p2 = 2 * SSM_STATE

    def swap(x):
        return jnp.concatenate([x[:, p2:], x[:, :p2]], axis=1)

    def body(c, carry):
        xf, zb = carry
        rf = pl.ds(pl.multiple_of(c * nb, nb), nb)
        rb = pl.ds(pl.multiple_of((nc - 1 - c) * nb, nb), nb)
        s_scr[rf, :p2] = xf[:, :p2]
        s_scr[rb, p2:] = zb[:, :p2]
        xf = arf * xf + aif * swap(xf) + e_scr[rf, :w2]
        zb = arb * zb + aib * swap(zb) + e_scr[rb, w2:]
        return xf, zb
    z0 = jnp.zeros((nb, w2), F32)
    lax.fori_loop(0, nc, body, (z0, z0))

    y = (jnp.dot(u, t_ref[0], preferred_element_type=F32)
         + jnp.dot(s_scr[...].astype(BF16), cm_ref[0], preferred_element_type=F32))
    y = y + d_ref[0] * u.astype(F32)
    o_ref[0] = _gelu_tanh(y).astype(BF16)


def _s5_branch(u, ops, d_tiled, batch, seq):
    t_op, bm, cm, a_r, a_i = ops
    nc = seq // S5_CHUNK
    rows = nc * batch
    ug = u.reshape(batch, nc, S5_CHUNK, SSM_GROUPS, SSM_GROUP)
    ug = jnp.transpose(ug, (3, 1, 0, 2, 4)).reshape(SSM_GROUPS, rows, S5_W)
    grp = lambda g: (g, 0, 0)
    y = pl.pallas_call(
        functools.partial(_s5_kernel, nc=nc, nb=batch),
        grid=(SSM_GROUPS,),
        in_specs=[
            pl.BlockSpec((1, rows, S5_W), grp),
            pl.BlockSpec((1, S5_W, S5_W), grp),
            pl.BlockSpec((1, S5_W, 2 * S5_W), grp),
            pl.BlockSpec((1, S5_W, S5_W), grp),
            pl.BlockSpec((1, 1, 2 * S5_W), grp),
            pl.BlockSpec((1, 1, 2 * S5_W), grp),
            pl.BlockSpec((1, 1, S5_W), grp),
        ],
        out_specs=pl.BlockSpec((1, rows, S5_W), grp),
        out_shape=jax.ShapeDtypeStruct((SSM_GROUPS, rows, S5_W), BF16),
        scratch_shapes=[pltpu.VMEM((rows, 2 * S5_W), F32), pltpu.VMEM((rows, S5_W), F32)],
        compiler_params=_cparams(("parallel",)),
        name="s5_chunked",
    )(ug, t_op, bm, cm, a_r, a_i, d_tiled)
    y = y.reshape(SSM_GROUPS, nc, batch, S5_CHUNK, SSM_GROUP)
    return jnp.transpose(y, (2, 1, 3, 0, 4)).reshape(batch * seq, D_SSM)


def _merge_kernel(y_ref, a_ref, gs_ref, ga_ref, wglu_ref, bglu_ref, wap_ref, wsp_ref, o_ref):
    y = y_ref[...]
    z = jnp.dot(y, wglu_ref[...], preferred_element_type=F32) + bglu_ref[...]
    act = (y.astype(F32) * _sigmoid(z)).astype(BF16)
    pa = jnp.dot(a_ref[...], wap_ref[...], preferred_element_type=F32)
    ps = jnp.dot(act, wsp_ref[...], preferred_element_type=F32)
    o = _sigmoid(ga_ref[...].astype(F32)) * pa + _sigmoid(gs_ref[...].astype(F32)) * ps
    o_ref[...] = o.astype(BF16)


def _merge(yact, attn, proj, wglu, bglu, wap, wsp):
    n = yact.shape[0]
    tm = min(256, n)
    gcol = (3 * ATTN_W + D_SSM) // D_MODEL
    const = lambda i: (0, 0)
    return pl.pallas_call(
        _merge_kernel,
        grid=(n // tm,),
        in_specs=[
            pl.BlockSpec((tm, D_SSM), lambda i: (i, 0)),
            pl.BlockSpec((tm, ATTN_W), lambda i: (i, 0)),
            pl.BlockSpec((tm, D_MODEL), lambda i: (i, gcol)),
            pl.BlockSpec((tm, D_MODEL), lambda i: (i, gcol + 1)),
            pl.BlockSpec((D_SSM, D_SSM), const),
            pl.BlockSpec((1, D_SSM), const),
            pl.BlockSpec((ATTN_W, D_MODEL), const),
            pl.BlockSpec((D_SSM, D_MODEL), const),
        ],
        out_specs=pl.BlockSpec((tm, D_MODEL), lambda i: (i, 0)),
        out_shape=jax.ShapeDtypeStruct((n, D_MODEL), BF16),
        compiler_params=_cparams(("parallel",)),
        name="glu_merge",
    )(yact, attn, proj, proj, wglu, bglu, wap, wsp)


def _outproj_kernel(x_ref, m_ref, wout_ref, gin_ref, bin_ref, g1_ref, b1_ref, wr_ref, br_ref,
                    x1_ref, xp_ref, te_ref, tg_ref):
    xn = _ln(x_ref[...], gin_ref[...], bin_ref[...])
    z = DN_ALPHA * xn + jnp.dot(m_ref[...], wout_ref[...], preferred_element_type=F32)
    x1 = _ln(z, g1_ref[...], b1_ref[...])
    x1_ref[...] = x1
    _store_row_tiles(xp_ref, 0, x1.shape[0], _pack_rows(x1))

    logits = jnp.dot(x1, wr_ref[...], preferred_element_type=F32,
                     precision=lax.Precision.HIGHEST) + br_ref[...]
    e_iota = lax.broadcasted_iota(I32, logits.shape, 1)
    k_iota = lax.broadcasted_iota(I32, (logits.shape[0], TOP_K), 1)
    vals = jnp.zeros((logits.shape[0], TOP_K), F32)
    idxs = jnp.zeros((logits.shape[0], TOP_K), I32)
    cur = logits
    for k in range(TOP_K):
        m = jnp.max(cur, axis=-1, keepdims=True)
        idx = jnp.min(jnp.where(cur == m, e_iota, N_EXPERTS), axis=-1, keepdims=True)
        vals = jnp.where(k_iota == k, m, vals)
        idxs = jnp.where(k_iota == k, idx, idxs)
        cur = jnp.where(e_iota == idx, -jnp.inf, cur)
    ex = jnp.exp(vals - jnp.max(vals, axis=-1, keepdims=True))
    te_ref[...] = idxs
    tg_ref[...] = ex / jnp.sum(ex, axis=-1, keepdims=True)


def _outproj(x, merged, wout, gin, bin_, g1, b1, wr, br):
    n = x.shape[0]
    tm = min(256, n)
    const = lambda i: (0, 0)
    row = lambda i: (i, 0)
    return pl.pallas_call(
        _outproj_kernel,
        grid=(n // tm,),
        in_specs=[
            pl.BlockSpec((tm, D_MODEL), row),
            pl.BlockSpec((tm, D_MODEL), row),
            pl.BlockSpec((D_MODEL, D_MODEL), const),
            pl.BlockSpec((1, D_MODEL), const),
            pl.BlockSpec((1, D_MODEL), const),
            pl.BlockSpec((1, D_MODEL), const),
            pl.BlockSpec((1, D_MODEL), const),
            pl.BlockSpec((D_MODEL, N_EXPERTS), const),
            pl.BlockSpec((1, N_EXPERTS), const),
        ],
        out_specs=[
            pl.BlockSpec((tm, D_MODEL), row),
            pl.BlockSpec((tm * ROW_TILE, LANES), row),
            pl.BlockSpec((tm, TOP_K), row),
            pl.BlockSpec((tm, TOP_K), row),
        ],
        out_shape=[
            jax.ShapeDtypeStruct((n, D_MODEL), F32),
            jax.ShapeDtypeStruct((n * ROW_TILE, LANES), U32),
            jax.ShapeDtypeStruct((n, TOP_K), I32),
            jax.ShapeDtypeStruct((n, TOP_K), F32),
        ],
        compiler_params=_cparams(("parallel",)),
        name="outproj_ln_router",
    )(x, merged, wout, gin, bin_, g1, b1, wr, br)


def _gather_rows(idx_smem, base, src_hbm, dst_vmem, sem, count):
    def body(i, carry):
        src_row = pl.multiple_of(idx_smem[base + i], ROW_TILE)
        dst_row = pl.multiple_of(i * ROW_TILE, ROW_TILE)
        pltpu.make_async_copy(src_hbm.at[pl.ds(src_row, ROW_TILE)], dst_vmem.at[pl.ds(dst_row, ROW_TILE)],
                              sem).start()
        return carry
    lax.fori_loop(0, count, body, 0, unroll=8)


def _wait_rows(src_hbm, dst_vmem, sem, count):
    pltpu.make_async_copy(src_hbm.at[pl.ds(0, count * ROW_TILE)], dst_vmem, sem).wait()


def _prefetch_gather(step, nsteps, idx_hbm, idx_smem, idx_sem, src_hbm, gbuf, gsem, count):
    slot = step % 2
    nxt = 1 - slot

    def idx_copy(s, sl):
        return pltpu.make_async_copy(idx_hbm.at[pl.ds(pl.multiple_of(s * count, count), count)],
                                     idx_smem.at[pl.ds(pl.multiple_of(sl * count, count), count)],
                                     idx_sem.at[sl])

    @pl.when(step == 0)
    def _():
        c = idx_copy(0, 0)
        c.start()
        c.wait()
        _gather_rows(idx_smem, 0, src_hbm, gbuf.at[0], gsem.at[0], count)

        @pl.when(nsteps > 1)
        def _():
            idx_copy(1, 1).start()

    _wait_rows(src_hbm, gbuf.at[slot], gsem.at[slot], count)

    @pl.when(step + 1 < nsteps)
    def _():
        idx_copy(step + 1, nxt).wait()
        _gather_rows(idx_smem, nxt * count, src_hbm, gbuf.at[nxt], gsem.at[nxt], count)

    @pl.when(step + 2 < nsteps)
    def _():
        idx_copy(step + 2, slot).start()


def _moe_kernel(be_ref, nu_ref, idx_hbm, x_hbm, wg_ref, bg_ref, wu_ref, bu_ref, wd_ref, bd_ref, o_ref,
                idx_smem, idx_sem, gbuf, gsem, xb_scr, acc_scr, *, tmb, rc):
    b = pl.program_id(0)
    f = pl.program_id(1)
    nf = pl.num_programs(1)
    n_used = nu_ref[0]
    active = b < n_used

    @pl.when(active & (f == 0))
    def _():
        _prefetch_gather(b, n_used, idx_hbm, idx_smem, idx_sem, x_hbm, gbuf, gsem, tmb)
        slot = b % 2

        def body(c, carry):
            r0 = pl.multiple_of(c * rc, rc)
            r = pl.ds(r0, rc)
            for s in range(ROW_TILE):
                lo, hi = _unpack_rows(_load_row_tile_words(gbuf.at[slot], r0, rc, s))
                xb_scr[r, s * LANES:(s + 1) * LANES] = lo.astype(BF16)
                xb_scr[r, HALF + s * LANES:HALF + (s + 1) * LANES] = hi.astype(BF16)
            return carry
        lax.fori_loop(0, tmb // rc, body, 0)

    @pl.when(active)
    def _():
        x = xb_scr[...]
        g = jnp.dot(x, wg_ref[0], preferred_element_type=F32) + bg_ref[0]
        u = jnp.dot(x, wu_ref[0], preferred_element_type=F32) + bu_ref[0]
        g = jnp.minimum(g, SWIGLU_LIMIT)
        u = jnp.clip(u, -SWIGLU_LIMIT, SWIGLU_LIMIT)
        h = (g * _sigmoid(SWIGLU_ALPHA * g) * (u + 1.0)).astype(BF16)
        part = jnp.dot(h, wd_ref[0], preferred_element_type=F32)

        @pl.when(f == 0)
        def _():
            acc_scr[...] = part + bd_ref[0]

        @pl.when(f > 0)
        def _():
            acc_scr[...] += part

    @pl.when(f == nf - 1)
    def _():
        @pl.when(active)
        def _():
            def body(c, carry):
                r0 = pl.multiple_of(c * rc, rc)
                _store_row_tiles(o_ref, r0, rc, _pack_rows(acc_scr[pl.ds(r0, rc), :]))
                return carry
            lax.fori_loop(0, tmb // rc, body, 0)

        @pl.when(jnp.logical_not(active))
        def _():
            o_ref[...] = jnp.zeros(o_ref.shape, U32)


def _moe_experts(block_e, n_used, row_tok, x_packed, wg, bg, wu, bu, wd, bd, tmb):
    nblk = row_tok.shape[0] // tmb
    tf = 256
    nf = D_FF // tf
    rc = min(256, tmb)

    def fsel(b, f, nu):
        return jnp.where(b < nu[0], f, nf - 1)

    grid_spec = pltpu.PrefetchScalarGridSpec(
        num_scalar_prefetch=2,
        grid=(nblk, nf),
        in_specs=[
            pl.BlockSpec(memory_space=pl.ANY),
            pl.BlockSpec(memory_space=pl.ANY),
            pl.BlockSpec((1, D_MODEL, tf), lambda b, f, be, nu: (be[b], 0, fsel(b, f, nu))),
            pl.BlockSpec((1, 1, tf), lambda b, f, be, nu: (be[b], 0, fsel(b, f, nu))),
            pl.BlockSpec((1, D_MODEL, tf), lambda b, f, be, nu: (be[b], 0, fsel(b, f, nu))),
            pl.BlockSpec((1, 1, tf), lambda b, f, be, nu: (be[b], 0, fsel(b, f, nu))),
            pl.BlockSpec((1, tf, D_MODEL), lambda b, f, be, nu: (be[b], fsel(b, f, nu), 0)),
            pl.BlockSpec((1, 1, D_MODEL), lambda b, f, be, nu: (be[b], 0, 0)),
        ],
        out_specs=pl.BlockSpec((tmb * ROW_TILE, LANES), lambda b, f, be, nu: (b, 0)),
        scratch_shapes=[
            pltpu.SMEM((2 * tmb,), I32),
            pltpu.SemaphoreType.DMA((2,)),
            pltpu.VMEM((2, tmb * ROW_TILE, LANES), U32),
            pltpu.SemaphoreType.DMA((2,)),
            pltpu.VMEM((tmb, D_MODEL), BF16),
            pltpu.VMEM((tmb, D_MODEL), F32),
        ],
    )
    return pl.pallas_call(
        functools.partial(_moe_kernel, tmb=tmb, rc=rc),
        grid_spec=grid_spec,
        out_shape=jax.ShapeDtypeStruct((nblk * tmb * ROW_TILE, LANES), U32),
        compiler_params=_cparams(("arbitrary", "arbitrary"), disable_bounds_checks=True),
        name="moe_experts",
    )(block_e, n_used, row_tok, x_packed, wg, bg, wu, bu, wd, bd)


def _combine_kernel(idx_hbm, y_hbm, x1_ref, gate_ref, g_ref, b_ref, o_ref,
                    idx_smem, idx_sem, gbuf, gsem, *, tc):
    i = pl.program_id(0)
    _prefetch_gather(i, pl.num_programs(0), idx_hbm, idx_smem, idx_sem, y_hbm, gbuf, gsem, TOP_K * tc)
    slot = i % 2
    gates = gate_ref[...]
    gk = [gates[:, k:k + 1] for k in range(TOP_K)]
    ncol = D_MODEL // LANES
    z = [DN_ALPHA * x1_ref[:, c * LANES:(c + 1) * LANES] for c in range(ncol)]
    for s in range(ROW_TILE):
        for k in range(TOP_K):
            lo, hi = _unpack_rows(_load_row_tile_words(gbuf.at[slot], k * tc, tc, s))
            z[s] = z[s] + gk[k] * lo
            z[ROW_TILE + s] = z[ROW_TILE + s] + gk[k] * hi
    mu = sum(jnp.sum(zc, axis=-1, keepdims=True) for zc in z) / D_MODEL
    z = [zc - mu for zc in z]
    var = sum(jnp.sum(zc * zc, axis=-1, keepdims=True) for zc in z) / D_MODEL
    inv = lax.rsqrt(var + LN_EPS)
    for c in range(ncol):
        cols = slice(c * LANES, (c + 1) * LANES)
        o_ref[:, cols] = z[c] * inv * g_ref[:, cols] + b_ref[:, cols]


def _combine(dest_tiles, yb, x1, gates, g2, b2, tc):
    n = x1.shape[0]
    const = lambda i: (0, 0)
    row = lambda i: (i, 0)
    return pl.pallas_call(
        functools.partial(_combine_kernel, tc=tc),
        grid=(n // tc,),
        in_specs=[
            pl.BlockSpec(memory_space=pl.ANY),
            pl.BlockSpec(memory_space=pl.ANY),
            pl.BlockSpec((tc, D_MODEL), row),
            pl.BlockSpec((tc, TOP_K), row),
            pl.BlockSpec((1, D_MODEL), const),
            pl.BlockSpec((1, D_MODEL), const),
        ],
        out_specs=pl.BlockSpec((tc, D_MODEL), row),
        out_shape=jax.ShapeDtypeStruct((n, D_MODEL), F32),
        scratch_shapes=[
            pltpu.SMEM((2 * TOP_K * tc,), I32),
            pltpu.SemaphoreType.DMA((2,)),
            pltpu.VMEM((2, TOP_K * tc * ROW_TILE, LANES), U32),
            pltpu.SemaphoreType.DMA((2,)),
        ],
        compiler_params=_cparams(("arbitrary",), disable_bounds_checks=True),
        name="moe_combine_ln",
    )(dest_tiles, yb, x1, gates, g2, b2)


def _route_meta(top_e, tmb):
    nt = top_e.shape[0]
    nblk = (nt * TOP_K) // tmb + N_EXPERTS
    e_ids = jnp.arange(N_EXPERTS, dtype=I32)
    onehot = (top_e[:, :, None] == e_ids[None, None, :]).sum(axis=1).astype(I32)
    incl = jnp.cumsum(onehot, axis=0)
    counts = incl[-1]
    rank = jnp.take_along_axis(incl - onehot, top_e, axis=1)
    padded = (counts + tmb - 1) // tmb * tmb
    pend = jnp.cumsum(padded)
    pstart = pend - padded
    start = jnp.cumsum(counts) - counts
    dest = (pstart[top_e] + rank).astype(I32)
    n_used = (pend[-1] // tmb).astype(I32)
    blk = jnp.arange(nblk, dtype=I32)
    be = jnp.clip(jnp.searchsorted(pend, blk * tmb, side='right'), 0, N_EXPERTS - 1).astype(I32)
    be = jnp.where(blk < n_used, be, be[jnp.maximum(n_used - 1, 0)])
    order = jnp.argsort(top_e.reshape(-1), stable=True).astype(I32)
    tok_sorted = order // TOP_K
    r = jnp.arange(nblk * tmb, dtype=I32)
    eb = be[r // tmb]
    off = r - pstart[eb]
    valid = (off < counts[eb]) & (r // tmb < n_used)
    src = jnp.clip(start[eb] + off, 0, nt * TOP_K - 1)
    row_tok = jnp.where(valid, tok_sorted[src], 0).astype(I32)
    return be, n_used.reshape(1), row_tok * ROW_TILE, dest * ROW_TILE


def kernel(x_prompt, x_sample, ln_in_g, ln_in_b, w_in, lam_q1, lam_k1, lam_q2, lam_k2, subln_g, w_attn_proj, ssm_lam_re, ssm_lam_im, ssm_log_dt, ssm_b_re, ssm_b_im, ssm_c_re, ssm_c_im, ssm_d, w_glu, b_glu, w_ssm_proj, w_out, ln1_g, ln1_b, w_router, b_router, w_gate, b_gate, w_up, b_up, w_down, b_down, ln2_g, ln2_b):
    l = 0
    row = lambda v: v.reshape(1, -1).astype(F32)
    lambda_init = 0.8 - 0.6 * math.exp(-0.3 * l)
    lam = (jnp.exp(jnp.sum(lam_q1[l].astype(F32) * lam_k1[l].astype(F32)))
           - jnp.exp(jnp.sum(lam_q2[l].astype(F32) * lam_k2[l].astype(F32))) + lambda_init).reshape(1, 1)
    w_in_b = w_in[l].astype(BF16)
    wglu_b = w_glu[l].astype(BF16)
    wap_b = w_attn_proj[l].astype(BF16)
    wsp_b = w_ssm_proj[l].astype(BF16)
    wout_b = w_out[l].astype(BF16)
    wg_b = w_gate[l].astype(BF16)
    wu_b = w_up[l].astype(BF16)
    wd_b = w_down[l].astype(BF16)
    s5_ops = _s5_operators(ssm_lam_re[l], ssm_lam_im[l], ssm_log_dt[l], ssm_b_re[l], ssm_b_im[l],
                           ssm_c_re[l], ssm_c_im[l])
    d_tiled = jnp.tile(ssm_d[l].astype(F32).reshape(SSM_GROUPS, 1, SSM_GROUP), (1, 1, S5_CHUNK))

    def pre_moe(x3):
        batch, seq, _ = x3.shape
        x = x3.reshape(batch * seq, D_MODEL)
        proj = _inproj(x, row(ln_in_g), row(ln_in_b), w_in_b)
        cos_t, sin_t = _rope_tables(seq)
        attn = _attention(proj, lam, cos_t, sin_t, row(subln_g[l]), batch, seq, lambda_init)
        u = proj[:, 3 * ATTN_W:3 * ATTN_W + D_SSM]
        yact = _s5_branch(u, s5_ops, d_tiled, batch, seq)
        merged = _merge(yact, attn, proj, wglu_b, row(b_glu[l]), wap_b, wsp_b)
        return _outproj(x, merged, wout_b, row(ln_in_g), row(ln_in_b), row(ln1_g[l]), row(ln1_b[l]),
                        w_router[l].astype(F32), row(b_router[l]))

    parts = [pre_moe(x_prompt), pre_moe(x_sample)]
    xp_all = jnp.concatenate([p[1] for p in parts], axis=0)
    te_all = jnp.concatenate([p[2] for p in parts], axis=0)
    nt = xp_all.shape[0]
    tmb = 1024 if nt >= 8192 else 128
    be, n_used, row_tok, dest = _route_meta(te_all, tmb)
    yb = _moe_experts(be, n_used, row_tok, xp_all, wg_b, b_gate[l].astype(F32)[:, None, :],
                      wu_b, b_up[l].astype(F32)[:, None, :], wd_b, b_down[l].astype(F32)[:, None, :], tmb)

    outs = []
    off = 0
    for x3, p in zip((x_prompt, x_sample), parts):
        n = p[0].shape[0]
        tc = min(256, n)
        d = dest[off:off + n].reshape(n // tc, tc, TOP_K)
        d = jnp.transpose(d, (0, 2, 1)).reshape(-1)
        out = _combine(d, yb, p[0], p[3], row(ln2_g[l]), row(ln2_b[l]), tc)
        outs.append(out.reshape(x3.shape))
        off += n
    return tuple(outs)
```

```python
import functools
import math

import jax
import jax.numpy as jnp
from jax import lax
from jax.experimental import pallas as pl
from jax.experimental.pallas import tpu as pltpu

F32 = jnp.float32
BF16 = jnp.bfloat16
U32 = jnp.uint32
I32 = jnp.int32

D_MODEL = 2048
DEPTH = 1
N_HEADS = 8
HEAD_DIM = 64
HEAD_W = 2 * HEAD_DIM
ATTN_W = N_HEADS * HEAD_W
ROT_DIM = HEAD_DIM // 4
ROPE_THETA = 500000.0
SSM_GROUP = 16
SSM_GROUPS = 64
D_SSM = SSM_GROUP * SSM_GROUPS
SSM_STATE = 64
S5_CHUNK = 16
S5_W = S5_CHUNK * SSM_GROUP
N_IN = 3 * ATTN_W + D_SSM + 2 * D_MODEL
N_EXPERTS = 32
TOP_K = 4
D_FF = D_MODEL
SWIGLU_LIMIT = 7.0
SWIGLU_ALPHA = 1.702
DN_ALPHA = (2.0 * DEPTH) ** 0.25
LN_EPS = 1e-5
HALF = D_MODEL // 2
LANES = 128
ROW_TILE = 8
MOE_TILE = 512

VMEM_LIMIT = 56 * 1024 * 1024


def _cparams(sem, **kw):
    return pltpu.CompilerParams(dimension_semantics=sem, vmem_limit_bytes=VMEM_LIMIT, **kw)


def _ln(x, g, b):
    mu = jnp.mean(x, axis=-1, keepdims=True)
    xc = x - mu
    var = jnp.mean(xc * xc, axis=-1, keepdims=True)
    return xc * lax.rsqrt(var + LN_EPS) * g + b


def _sigmoid(x):
    return 1.0 / (1.0 + jnp.exp(-x))


def _gelu_tanh(x):
    return 0.5 * x * (1.0 + jnp.tanh(math.sqrt(2.0 / math.pi) * (x + 0.044715 * (x * x * x))))


def _pack_words(lo, hi):
    lo = lax.bitcast_convert_type(lo.astype(BF16).astype(F32), U32)
    hi = lax.bitcast_convert_type(hi.astype(BF16).astype(F32), U32)
    return (hi & jnp.uint32(0xFFFF0000)) | (lo >> 16)


def _pack_rows(x):
    return _pack_words(x[:, :HALF], x[:, HALF:])


def _packed_chunks(s, tile):
    sub_per_tile = tile // (2 * LANES)
    n, q = divmod(s, sub_per_tile)
    lo = n * (tile // LANES) + q
    return lo, lo + sub_per_tile


def _unpack_rows(w):
    lo = lax.bitcast_convert_type(w << 16, F32)
    hi = lax.bitcast_convert_type(w & jnp.uint32(0xFFFF0000), F32)
    return lo, hi


def _store_row_tiles(ref, r0, nrows, packed):
    for s in range(ROW_TILE):
        ref[pl.ds(r0 * ROW_TILE + s, nrows, stride=ROW_TILE), :] = packed[:, s * LANES:(s + 1) * LANES]


def _load_row_tile_words(ref, r0, nrows, s):
    return ref[pl.ds(r0 * ROW_TILE + s, nrows, stride=ROW_TILE), :]


def _inproj_kernel(x_ref, g_ref, b_ref, w_ref, o_ref, xn_ref, *, tm, rc):
    @pl.when(pl.program_id(1) == 0)
    def _():
        def body(c, carry):
            r = pl.ds(pl.multiple_of(c * rc, rc), rc)
            xn_ref[r, :] = _ln(x_ref[r, :], g_ref[...], b_ref[...]).astype(BF16)
            return carry
        lax.fori_loop(0, tm // rc, body, 0)

    o_ref[...] = jnp.dot(xn_ref[...], w_ref[...], preferred_element_type=F32).astype(BF16)


def _inproj(x, g, b, w_bf16):
    n = x.shape[0]
    tm = min(1024, n)
    tn = 512
    rc = min(256, tm)
    return pl.pallas_call(
        functools.partial(_inproj_kernel, tm=tm, rc=rc),
        grid=(n // tm, N_IN // tn),
        in_specs=[
            pl.BlockSpec((tm, D_MODEL), lambda i, j: (i, 0)),
            pl.BlockSpec((1, D_MODEL), lambda i, j: (0, 0)),
            pl.BlockSpec((1, D_MODEL), lambda i, j: (0, 0)),
            pl.BlockSpec((D_MODEL, tn), lambda i, j: (0, j)),
        ],
        out_specs=pl.BlockSpec((tm, tn), lambda i, j: (i, j)),
        out_shape=jax.ShapeDtypeStruct((n, N_IN), BF16),
        scratch_shapes=[pltpu.VMEM((tm, D_MODEL), BF16)],
        compiler_params=_cparams(("parallel", "arbitrary")),
        name="ln_inproj",
    )(x, g, b, w_bf16)


def _rope(x, c, s):
    lane = lax.broadcasted_iota(I32, (1, HEAD_W), 1)
    first_half = (lane % HEAD_DIM) < (ROT_DIM // 2)
    xs = jnp.where(first_half, pltpu.roll(x, HEAD_W - ROT_DIM // 2, 1), pltpu.roll(x, ROT_DIM // 2, 1))
    return x * c + xs * s


def _attn_kernel(lam_ref, q_ref, k_ref, v_ref, cq_ref, sq_ref, ck_ref, sk_ref, g_ref, o_ref,
                 k_scr, vt_scr, q_scr, m_scr, l_scr, acc_scr, *, seq, tq, tk, out_scale):
    @pl.when(pl.program_id(2) == 0)
    def _():
        def body(c, carry):
            r = pl.ds(pl.multiple_of(c * tk, tk), tk)
            k_scr[r, :] = _rope(k_ref[r, :].astype(F32), ck_ref[r, :], sk_ref[r, :]).astype(BF16)
            vt_scr[c] = v_ref[r, :].astype(F32).T.astype(BF16)
            return carry
        lax.fori_loop(0, seq // tk, body, 0)

    q = _rope(q_ref[...].astype(F32), cq_ref[...], sq_ref[...]) * (HEAD_DIM ** -0.5 * math.log2(math.e))
    lane = lax.broadcasted_iota(I32, (1, HEAD_W), 1)
    comp1 = lane < HEAD_DIM
    q_scr[pl.ds(0, tq), :] = jnp.where(comp1, q, 0.0).astype(BF16)
    q_scr[pl.ds(tq, tq), :] = jnp.where(comp1, 0.0, q).astype(BF16)
    m_scr[...] = jnp.full(m_scr.shape, -jnp.inf, F32)
    l_scr[...] = jnp.zeros(l_scr.shape, F32)
    acc_scr[...] = jnp.zeros(acc_scr.shape, F32)

    def body(j, carry):
        r = pl.ds(pl.multiple_of(j * tk, tk), tk)
        st = lax.dot_general(k_scr[r, :], q_scr[...], (((1,), (1,)), ((), ())),
                             preferred_element_type=F32)
        m_old = m_scr[...]
        m_new = jnp.maximum(m_old, jnp.max(st, axis=0, keepdims=True))
        p = jnp.exp2(st - m_new)
        a = jnp.exp2(m_old - m_new)
        l_scr[...] = a * l_scr[...] + jnp.sum(p, axis=0, keepdims=True)
        acc_scr[...] = a * acc_scr[...] + jnp.dot(vt_scr[j], p.astype(BF16), preferred_element_type=F32)
        m_scr[...] = m_new
        return carry
    lax.fori_loop(0, seq // tk, body, 0)

    o = acc_scr[...] / l_scr[...]
    o = o[:, :tq] - lam_ref[0, 0] * o[:, tq:]
    o = o * lax.rsqrt(jnp.mean(o * o, axis=0, keepdims=True) + LN_EPS) * g_ref[...] * out_scale
    o_ref[...] = o.T.astype(BF16)


def _attention(proj, lam, cos_t, sin_t, subln_g, batch, seq, lambda_init):
    tq = min(256, seq)
    tk = min(1024, seq)
    nq = seq // tq
    kern = functools.partial(_attn_kernel, seq=seq, tq=tq, tk=tk, out_scale=1.0 - lambda_init)
    return pl.pallas_call(
        kern,
        grid=(batch, N_HEADS, nq),
        in_specs=[
            pl.BlockSpec(memory_space=pltpu.SMEM),
            pl.BlockSpec((tq, HEAD_W), lambda b, h, i: (b * nq + i, h)),
            pl.BlockSpec((seq, HEAD_W), lambda b, h, i: (b, N_HEADS + h)),
            pl.BlockSpec((seq, HEAD_W), lambda b, h, i: (b, 2 * N_HEADS + h)),
            pl.BlockSpec((tq, HEAD_W), lambda b, h, i: (i, 0)),
            pl.BlockSpec((tq, HEAD_W), lambda b, h, i: (i, 0)),
            pl.BlockSpec((seq, HEAD_W), lambda b, h, i: (0, 0)),
            pl.BlockSpec((seq, HEAD_W), lambda b, h, i: (0, 0)),
            pl.BlockSpec((HEAD_W, 1), lambda b, h, i: (0, 0)),
        ],
        out_specs=pl.BlockSpec((tq, HEAD_W), lambda b, h, i: (b * nq + i, h)),
        out_shape=jax.ShapeDtypeStruct((batch * seq, ATTN_W), BF16),
        scratch_shapes=[
            pltpu.VMEM((seq, HEAD_W), BF16),
            pltpu.VMEM((seq // tk, HEAD_W, tk), BF16),
            pltpu.VMEM((2 * tq, HEAD_W), BF16),
            pltpu.VMEM((1, 2 * tq), F32),
            pltpu.VMEM((1, 2 * tq), F32),
            pltpu.VMEM((HEAD_W, 2 * tq), F32),
        ],
        compiler_params=_cparams(("parallel", "parallel", "arbitrary")),
        name="diff_attention",
    )(lam, proj, proj, proj, cos_t, sin_t, cos_t, sin_t, subln_g.reshape(HEAD_W, 1))


def _rope_tables(seq):
    half = ROT_DIM // 2
    inv = ROPE_THETA ** (-jnp.arange(0, ROT_DIM, 2, dtype=F32) / ROT_DIM)
    ang = jnp.arange(seq, dtype=F32)[:, None] * inv[None, :]
    lane = jnp.arange(HEAD_W) % HEAD_DIM
    cos_l = jnp.cos(ang)[:, lane % half]
    sin_l = jnp.sin(ang)[:, lane % half]
    cos_t = jnp.where(lane[None, :] < ROT_DIM, cos_l, 1.0)
    sin_t = jnp.where(lane[None, :] < half, -sin_l, jnp.where(lane[None, :] < ROT_DIM, sin_l, 0.0))
    return cos_t.astype(F32), sin_t.astype(F32)


def _s5_operators(lam_re, lam_im, log_dt, b_re, b_im, c_re, c_im):
    L, G, P, CG = S5_CHUNK, SSM_GROUPS, SSM_STATE, SSM_GROUP
    hp = lax.Precision.HIGHEST
    lr = jnp.minimum(lam_re.astype(F32), -1e-4)
    li = lam_im.astype(F32)
    dt = jnp.exp(log_dt.astype(F32))[..., None]
    nn = jnp.arange(L + 1, dtype=F32)[:, None, None, None]
    mag = jnp.exp(nn * (lr * dt)[None])
    pw_r = mag * jnp.cos(nn * (li * dt)[None])
    pw_i = mag * jnp.sin(nn * (li * dt)[None])
    ab_r, ab_i = pw_r[1], pw_i[1]
    nr = ab_r - 1.0
    den = lr * lr + li * li
    cr_ = ((nr * lr + ab_i * li) / den)[..., None]
    ci_ = ((ab_i * lr - nr * li) / den)[..., None]
    br = b_re.astype(F32)
    bi = b_im.astype(F32)
    bb_r = cr_ * br - ci_ * bi
    bb_i = cr_ * bi + ci_ * br
    cr = c_re.astype(F32)
    ci = c_im.astype(F32)

    m_r = pw_r[..., None] * bb_r[None] - pw_i[..., None] * bb_i[None]
    m_i = pw_r[..., None] * bb_i[None] + pw_i[..., None] * bb_r[None]
    kk = (jnp.einsum('dgop,ndgpi->ndgoi', cr, m_r, precision=hp)
          - jnp.einsum('dgop,ndgpi->ndgoi', ci, m_i, precision=hp))
    ii = jnp.arange(L)[:, None]
    jj = jnp.arange(L)[None, :]
    kf = jnp.where((jj >= ii)[:, :, None, None, None], kk[:, 0][jnp.clip(jj - ii, 0, L)], 0.0)
    kb = jnp.where((ii >= jj)[:, :, None, None, None], kk[:, 1][jnp.clip(ii - jj, 0, L)], 0.0)
    t_op = jnp.transpose(kf + kb, (2, 0, 4, 1, 3)).reshape(G, S5_W, S5_W)

    def state_in(mr, mi):
        e_re = jnp.transpose(mr, (1, 0, 3, 2)).reshape(G, S5_W, P)
        e_im = jnp.transpose(mi, (1, 0, 3, 2)).reshape(G, S5_W, P)
        return jnp.concatenate([e_re, e_im, e_im, e_re], axis=-1)
    bm = jnp.concatenate([state_in(m_r[L - 1::-1, 0][:L], m_i[L - 1::-1, 0][:L]),
                          state_in(m_r[:L, 1], m_i[:L, 1])], axis=-1)

    def state_out(d, pows):
        pr = pw_r[pows, d]
        pi = pw_i[pows, d]
        g_r = cr[d][None] * pr[:, :, None, :] - ci[d][None] * pi[:, :, None, :]
        g_i = cr[d][None] * pi[:, :, None, :] + ci[d][None] * pr[:, :, None, :]
        rows_re = jnp.transpose(g_r, (1, 3, 0, 2)).reshape(G, P, S5_W)
        rows_im = jnp.transpose(-g_i, (1, 3, 0, 2)).reshape(G, P, S5_W)
        return jnp.concatenate([rows_re, rows_im], axis=1)
    cm = jnp.concatenate([state_out(0, jnp.arange(1, L + 1)), state_out(1, jnp.arange(L, 0, -1))], axis=1)

    def step_mult(d):
        ar = jnp.concatenate([pw_r[L, d]] * 4, axis=-1)
        ai = jnp.concatenate([-pw_i[L, d], pw_i[L, d], pw_i[L, d], -pw_i[L, d]], axis=-1)
        return ar, ai
    arf, aif = step_mult(0)
    arb, aib = step_mult(1)
    a_r = jnp.concatenate([arf, arb], axis=-1)[:, None, :]
    a_i = jnp.concatenate([aif, aib], axis=-1)[:, None, :]
    return t_op.astype(BF16), bm.astype(BF16), cm.astype(BF16), a_r, a_i


def _s5_kernel(u_ref, t_ref, bm_ref, cm_ref, ar_ref, ai_ref, d_ref, o_ref, e_scr, s_scr, *, nc, nb):
    u = u_ref[0]
    e_scr[...] = jnp.dot(u, bm_ref[0], preferred_element_type=F32)
    w2 = 2 * S5_W // 2
    arf = ar_ref[0, :, :w2]
    aif = ai_ref[0, :, :w2]
    arb = ar_ref[0, :, w2:]
    aib = ai_ref[0, :, w2:]
    p2 = 2 * SSM_STATE

    def swap(x):
        return jnp.concatenate([x[:, p2:], x[:, :p2]], axis=1)

    def body(c, carry):
        xf, zb = carry
        rf = pl.ds(pl.multiple_of(c * nb, nb), nb)
        rb = pl.ds(pl.multiple_of((nc - 1 - c) * nb, nb), nb)
        s_scr[rf, :p2] = xf[:, :p2]
        s_scr[rb, p2:] = zb[:, :p2]
        xf = arf * xf + aif * swap(xf) + e_scr[rf, :w2]
        zb = arb * zb + aib * swap(zb) + e_scr[rb, w2:]
        return xf, zb
    z0 = jnp.zeros((nb, w2), F32)
    lax.fori_loop(0, nc, body, (z0, z0))

    y = (jnp.dot(u, t_ref[0], preferred_element_type=F32)
         + jnp.dot(s_scr[...].astype(BF16), cm_ref[0], preferred_element_type=F32))
    y = y + d_ref[0] * u.astype(F32)
    o_ref[0] = _gelu_tanh(y).astype(BF16)


def _s5_branch(u, ops, d_tiled, batch, seq):
    t_op, bm, cm, a_r, a_i = ops
    nc = seq // S5_CHUNK
    rows = nc * batch
    ug = u.reshape(batch, nc, S5_CHUNK, SSM_GROUPS, SSM_GROUP)
    ug = jnp.transpose(ug, (3, 1, 0, 2, 4)).reshape(SSM_GROUPS, rows, S5_W)
    grp = lambda g: (g, 0, 0)
    y = pl.pallas_call(
        functools.partial(_s5_kernel, nc=nc, nb=batch),
        grid=(SSM_GROUPS,),
        in_specs=[
            pl.BlockSpec((1, rows, S5_W), grp),
            pl.BlockSpec((1, S5_W, S5_W), grp),
            pl.BlockSpec((1, S5_W, 2 * S5_W), grp),
            pl.BlockSpec((1, S5_W, S5_W), grp),
            pl.BlockSpec((1, 1, 2 * S5_W), grp),
            pl.BlockSpec((1, 1, 2 * S5_W), grp),
            pl.BlockSpec((1, 1, S5_W), grp),
        ],
        out_specs=pl.BlockSpec((1, rows, S5_W), grp),
        out_shape=jax.ShapeDtypeStruct((SSM_GROUPS, rows, S5_W), BF16),
        scratch_shapes=[pltpu.VMEM((rows, 2 * S5_W), F32), pltpu.VMEM((rows, S5_W), F32)],
        compiler_params=_cparams(("parallel",)),
        name="s5_chunked",
    )(ug, t_op, bm, cm, a_r, a_i, d_tiled)
    y = y.reshape(SSM_GROUPS, nc, batch, S5_CHUNK, SSM_GROUP)
    return jnp.transpose(y, (2, 1, 3, 0, 4)).reshape(batch * seq, D_SSM)


def _merge_kernel(y_ref, a_ref, gs_ref, ga_ref, wglu_ref, bglu_ref, wap_ref, wsp_ref, o_ref):
    y = y_ref[...]
    z = jnp.dot(y, wglu_ref[...], preferred_element_type=F32) + bglu_ref[...]
    act = (y.astype(F32) * _sigmoid(z)).astype(BF16)
    pa = jnp.dot(a_ref[...], wap_ref[...], preferred_element_type=F32)
    ps = jnp.dot(act, wsp_ref[...], preferred_element_type=F32)
    o = _sigmoid(ga_ref[...].astype(F32)) * pa + _sigmoid(gs_ref[...].astype(F32)) * ps
    o_ref[...] = o.astype(BF16)


def _merge(yact, attn, proj, wglu, bglu, wap, wsp):
    n = yact.shape[0]
    tm = min(256, n)
    gcol = (3 * ATTN_W + D_SSM) // D_MODEL
    const = lambda i: (0, 0)
    return pl.pallas_call(
        _merge_kernel,
        grid=(n // tm,),
        in_specs=[
            pl.BlockSpec((tm, D_SSM), lambda i: (i, 0)),
            pl.BlockSpec((tm, ATTN_W), lambda i: (i, 0)),
            pl.BlockSpec((tm, D_MODEL), lambda i: (i, gcol)),
            pl.BlockSpec((tm, D_MODEL), lambda i: (i, gcol + 1)),
            pl.BlockSpec((D_SSM, D_SSM), const),
            pl.BlockSpec((1, D_SSM), const),
            pl.BlockSpec((ATTN_W, D_MODEL), const),
            pl.BlockSpec((D_SSM, D_MODEL), const),
        ],
        out_specs=pl.BlockSpec((tm, D_MODEL), lambda i: (i, 0)),
        out_shape=jax.ShapeDtypeStruct((n, D_MODEL), BF16),
        compiler_params=_cparams(("parallel",)),
        name="glu_merge",
    )(yact, attn, proj, proj, wglu, bglu, wap, wsp)


def _outproj_kernel(x_ref, m_ref, wout_ref, gin_ref, bin_ref, g1_ref, b1_ref, wr_ref, br_ref,
                    x1_ref, xp_ref, te_ref, tg_ref):
    xn = _ln(x_ref[...], gin_ref[...], bin_ref[...])
    z = DN_ALPHA * xn + jnp.dot(m_ref[...], wout_ref[...], preferred_element_type=F32)
    x1 = _ln(z, g1_ref[...], b1_ref[...])
    x1_ref[...] = x1
    _store_row_tiles(xp_ref, 0, x1.shape[0], _pack_rows(x1))

    x_hi = x1.astype(BF16)
    x_lo = (x1 - x_hi.astype(F32)).astype(BF16)
    r = (jnp.dot(x_hi, wr_ref[...], preferred_element_type=F32)
         + jnp.dot(x_lo, wr_ref[...], preferred_element_type=F32))
    logits = r[:, :N_EXPERTS] + r[:, N_EXPERTS:] + br_ref[...]
    e_iota = lax.broadcasted_iota(I32, logits.shape, 1)
    k_iota = lax.broadcasted_iota(I32, (logits.shape[0], TOP_K), 1)
    vals = jnp.zeros((logits.shape[0], TOP_K), F32)
    idxs = jnp.zeros((logits.shape[0], TOP_K), I32)
    cur = logits
    for k in range(TOP_K):
        m = jnp.max(cur, axis=-1, keepdims=True)
        idx = jnp.min(jnp.where(cur == m, e_iota, N_EXPERTS), axis=-1, keepdims=True)
        vals = jnp.where(k_iota == k, m, vals)
        idxs = jnp.where(k_iota == k, idx, idxs)
        cur = jnp.where(e_iota == idx, -jnp.inf, cur)
    ex = jnp.exp(vals - jnp.max(vals, axis=-1, keepdims=True))
    te_ref[...] = idxs
    tg_ref[...] = ex / jnp.sum(ex, axis=-1, keepdims=True)


def _outproj(x, merged, wout, gin, bin_, g1, b1, wr, br):
    n = x.shape[0]
    tm = min(256, n)
    const = lambda i: (0, 0)
    row = lambda i: (i, 0)
    return pl.pallas_call(
        _outproj_kernel,
        grid=(n // tm,),
        in_specs=[
            pl.BlockSpec((tm, D_MODEL), row),
            pl.BlockSpec((tm, D_MODEL), row),
            pl.BlockSpec((D_MODEL, D_MODEL), const),
            pl.BlockSpec((1, D_MODEL), const),
            pl.BlockSpec((1, D_MODEL), const),
            pl.BlockSpec((1, D_MODEL), const),
            pl.BlockSpec((1, D_MODEL), const),
            pl.BlockSpec((D_MODEL, 2 * N_EXPERTS), const),
            pl.BlockSpec((1, N_EXPERTS), const),
        ],
        out_specs=[
            pl.BlockSpec((tm, D_MODEL), row),
            pl.BlockSpec((tm * ROW_TILE, LANES), row),
            pl.BlockSpec((tm, TOP_K), row),
            pl.BlockSpec((tm, TOP_K), row),
        ],
        out_shape=[
            jax.ShapeDtypeStruct((n, D_MODEL), F32),
            jax.ShapeDtypeStruct((n * ROW_TILE, LANES), U32),
            jax.ShapeDtypeStruct((n, TOP_K), I32),
            jax.ShapeDtypeStruct((n, TOP_K), F32),
        ],
        compiler_params=_cparams(("parallel",)),
        name="outproj_ln_router",
    )(x, merged, wout, gin, bin_, g1, b1, wr, br)


def _gather_rows(idx_smem, base, src_hbm, dst_vmem, sem, count):
    def body(i, carry):
        src_row = pl.multiple_of(idx_smem[base + i], ROW_TILE)
        dst_row = pl.multiple_of(i * ROW_TILE, ROW_TILE)
        pltpu.make_async_copy(src_hbm.at[pl.ds(src_row, ROW_TILE)], dst_vmem.at[pl.ds(dst_row, ROW_TILE)],
                              sem).start()
        return carry
    lax.fori_loop(0, count, body, 0, unroll=8)


def _wait_rows(src_hbm, dst_vmem, sem, count):
    pltpu.make_async_copy(src_hbm.at[pl.ds(0, count * ROW_TILE)], dst_vmem, sem).wait()


def _prefetch_gather(step, nsteps, idx_hbm, idx_smem, idx_sem, src_hbm, gbuf, gsem, count):
    slot = step % 2
    nxt = 1 - slot

    def idx_copy(s, sl):
        return pltpu.make_async_copy(idx_hbm.at[pl.ds(pl.multiple_of(s * count, count), count)],
                                     idx_smem.at[pl.ds(pl.multiple_of(sl * count, count), count)],
                                     idx_sem.at[sl])

    @pl.when(step == 0)
    def _():
        c = idx_copy(0, 0)
        c.start()
        c.wait()
        _gather_rows(idx_smem, 0, src_hbm, gbuf.at[0], gsem.at[0], count)

        @pl.when(nsteps > 1)
        def _():
            idx_copy(1, 1).start()

    _wait_rows(src_hbm, gbuf.at[slot], gsem.at[slot], count)

    @pl.when(step + 1 < nsteps)
    def _():
        idx_copy(step + 1, nxt).wait()
        _gather_rows(idx_smem, nxt * count, src_hbm, gbuf.at[nxt], gsem.at[nxt], count)

    @pl.when(step + 2 < nsteps)
    def _():
        idx_copy(step + 2, slot).start()


def _moe_kernel(be_ref, nu_ref, idx_hbm, x_hbm, wg_ref, bg_ref, wu_ref, bu_ref, wd_ref, bd_ref, o_ref,
                idx_smem, idx_sem, gbuf, gsem, xb_scr, h_scr, *, tmb, rc, nf, nn, tf, tn):
    b = pl.program_id(0)
    t = pl.program_id(1)
    n_used = nu_ref[0]
    active = b < n_used
    slot = b % 2
    nxt = 1 - slot
    per = tmb // (nf + nn)

    def idx_copy(s, sl):
        return pltpu.make_async_copy(idx_hbm.at[pl.ds(pl.multiple_of(s * tmb, tmb), tmb)],
                                     idx_smem.at[pl.ds(pl.multiple_of(sl * tmb, tmb), tmb)],
                                     idx_sem.at[sl])

    def gather_next_slice():
        base = nxt * tmb + t * per
        for i in range(per):
            src_row = pl.multiple_of(idx_smem[base + i], ROW_TILE)
            dst_row = pl.multiple_of((t * per + i) * ROW_TILE, ROW_TILE)
            pltpu.make_async_copy(x_hbm.at[pl.ds(src_row, ROW_TILE)], gbuf.at[nxt, pl.ds(dst_row, ROW_TILE)],
                                  gsem.at[nxt]).start()

    @pl.when((b <= n_used) & (t == 0))
    def _():
        @pl.when(b == 0)
        def _():
            c = idx_copy(0, 0)
            c.start()
            c.wait()
            _gather_rows(idx_smem, 0, x_hbm, gbuf.at[0], gsem.at[0], tmb)
            idx_copy(1, 1).start()

        _wait_rows(x_hbm, gbuf.at[slot], gsem.at[slot], tmb)

    @pl.when(active & (t == 0))
    def _():
        idx_copy(b + 1, nxt).wait()

        @pl.when(b + 2 <= n_used)
        def _():
            idx_copy(b + 2, slot).start()

        def body(c, carry):
            r0 = pl.multiple_of(c * rc, rc)
            r = pl.ds(r0, rc)
            for s in range(ROW_TILE):
                c_lo, c_hi = _packed_chunks(s, D_MODEL)
                lo, hi = _unpack_rows(_load_row_tile_words(gbuf.at[slot], r0, rc, s))
                xb_scr[r, c_lo * LANES:(c_lo + 1) * LANES] = lo.astype(BF16)
                xb_scr[r, c_hi * LANES:(c_hi + 1) * LANES] = hi.astype(BF16)
            return carry
        lax.fori_loop(0, tmb // rc, body, 0)

    @pl.when(active & (t < nf))
    def _():
        gather_next_slice()
        x = xb_scr[...]
        g = jnp.dot(x, wg_ref[0], preferred_element_type=F32) + bg_ref[0]
        u = jnp.dot(x, wu_ref[0], preferred_element_type=F32) + bu_ref[0]
        g = jnp.minimum(g, SWIGLU_LIMIT)
        u = jnp.clip(u, -SWIGLU_LIMIT, SWIGLU_LIMIT)
        h = (g * _sigmoid(SWIGLU_ALPHA * g) * (u + 1.0)).astype(BF16)
        for f in range(nf):
            @pl.when(t == f)
            def _():
                h_scr[:, f * tf:(f + 1) * tf] = h

    @pl.when(active & (t >= nf))
    def _():
        gather_next_slice()
        y = jnp.dot(h_scr[...], wd_ref[0], preferred_element_type=F32) + bd_ref[0]
        sub_per_tile = tn // (2 * LANES)
        s0 = (t - nf) * sub_per_tile
        for c in range(tmb // rc):
            yc = y[c * rc:(c + 1) * rc]
            for q in range(sub_per_tile):
                lo = yc[:, q * LANES:(q + 1) * LANES]
                hi = yc[:, tn // 2 + q * LANES:tn // 2 + (q + 1) * LANES]
                o_ref[pl.ds(c * rc * ROW_TILE + s0 + q, rc, stride=ROW_TILE), :] = _pack_words(lo, hi)

    @pl.when(jnp.logical_not(active) & (t == nf + nn - 1))
    def _():
        o_ref[...] = jnp.zeros(o_ref.shape, U32)


def _moe_experts(block_e, n_used, row_tok, x_packed, wg, bg, wu, bu, wd, bd, tmb):
    nblk = row_tok.shape[0] // tmb
    tf = MOE_TILE
    tn = MOE_TILE
    nf = D_FF // tf
    nn = D_MODEL // tn
    rc = min(256, tmb)

    def fsel(b, t, nu):
        return jnp.where(b < nu[0], jnp.minimum(t, nf - 1), nf - 1)

    def nsel(b, t, nu):
        return jnp.where(b < nu[0], jnp.maximum(t - nf, 0), nn - 1)

    grid_spec = pltpu.PrefetchScalarGridSpec(
        num_scalar_prefetch=2,
        grid=(nblk, nf + nn),
        in_specs=[
            pl.BlockSpec(memory_space=pl.ANY),
            pl.BlockSpec(memory_space=pl.ANY),
            pl.BlockSpec((1, D_MODEL, tf), lambda b, t, be, nu: (be[b], 0, fsel(b, t, nu))),
            pl.BlockSpec((1, 1, tf), lambda b, t, be, nu: (be[b], 0, fsel(b, t, nu))),
            pl.BlockSpec((1, D_MODEL, tf), lambda b, t, be, nu: (be[b], 0, fsel(b, t, nu))),
            pl.BlockSpec((1, 1, tf), lambda b, t, be, nu: (be[b], 0, fsel(b, t, nu))),
            pl.BlockSpec((1, D_FF, tn), lambda b, t, be, nu: (be[b], 0, nsel(b, t, nu))),
            pl.BlockSpec((1, 1, tn), lambda b, t, be, nu: (be[b], 0, nsel(b, t, nu))),
        ],
        out_specs=pl.BlockSpec((tmb * ROW_TILE, LANES), lambda b, t, be, nu: (b, 0)),
        scratch_shapes=[
            pltpu.SMEM((2 * tmb,), I32),
            pltpu.SemaphoreType.DMA((2,)),
            pltpu.VMEM((2, tmb * ROW_TILE, LANES), U32),
            pltpu.SemaphoreType.DMA((2,)),
            pltpu.VMEM((tmb, D_MODEL), BF16),
            pltpu.VMEM((tmb, D_FF), BF16),
        ],
    )
    return pl.pallas_call(
        functools.partial(_moe_kernel, tmb=tmb, rc=rc, nf=nf, nn=nn, tf=tf, tn=tn),
        grid_spec=grid_spec,
        out_shape=jax.ShapeDtypeStruct((nblk * tmb * ROW_TILE, LANES), U32),
        compiler_params=_cparams(("arbitrary", "arbitrary"), disable_bounds_checks=True),
        name="moe_experts",
    )(block_e, n_used, row_tok, x_packed, wg, bg, wu, bu, wd, bd)


def _combine_kernel(idx_hbm, y_hbm, x1_ref, gate_ref, g_ref, b_ref, o_ref,
                    idx_smem, idx_sem, gbuf, gsem, *, tc):
    i = pl.program_id(0)
    _prefetch_gather(i, pl.num_programs(0), idx_hbm, idx_smem, idx_sem, y_hbm, gbuf, gsem, TOP_K * tc)
    slot = i % 2
    gates = gate_ref[...]
    gk = [gates[:, k:k + 1] for k in range(TOP_K)]
    ncol = D_MODEL // LANES
    z = [DN_ALPHA * x1_ref[:, c * LANES:(c + 1) * LANES] for c in range(ncol)]
    for s in range(ROW_TILE):
        c_lo, c_hi = _packed_chunks(s, MOE_TILE)
        for k in range(TOP_K):
            lo, hi = _unpack_rows(_load_row_tile_words(gbuf.at[slot], k * tc, tc, s))
            z[c_lo] = z[c_lo] + gk[k] * lo
            z[c_hi] = z[c_hi] + gk[k] * hi
    mu = sum(jnp.sum(zc, axis=-1, keepdims=True) for zc in z) / D_MODEL
    z = [zc - mu for zc in z]
    var = sum(jnp.sum(zc * zc, axis=-1, keepdims=True) for zc in z) / D_MODEL
    inv = lax.rsqrt(var + LN_EPS)
    for c in range(ncol):
        cols = slice(c * LANES, (c + 1) * LANES)
        o_ref[:, cols] = z[c] * inv * g_ref[:, cols] + b_ref[:, cols]


def _combine(dest_tiles, yb, x1, gates, g2, b2, tc):
    n = x1.shape[0]
    const = lambda i: (0, 0)
    row = lambda i: (i, 0)
    return pl.pallas_call(
        functools.partial(_combine_kernel, tc=tc),
        grid=(n // tc,),
        in_specs=[
            pl.BlockSpec(memory_space=pl.ANY),
            pl.BlockSpec(memory_space=pl.ANY),
            pl.BlockSpec((tc, D_MODEL), row),
            pl.BlockSpec((tc, TOP_K), row),
            pl.BlockSpec((1, D_MODEL), const),
            pl.BlockSpec((1, D_MODEL), const),
        ],
        out_specs=pl.BlockSpec((tc, D_MODEL), row),
        out_shape=jax.ShapeDtypeStruct((n, D_MODEL), F32),
        scratch_shapes=[
            pltpu.SMEM((2 * TOP_K * tc,), I32),
            pltpu.SemaphoreType.DMA((2,)),
            pltpu.VMEM((2, TOP_K * tc * ROW_TILE, LANES), U32),
            pltpu.SemaphoreType.DMA((2,)),
        ],
        compiler_params=_cparams(("arbitrary",), disable_bounds_checks=True),
        name="moe_combine_ln",
    )(dest_tiles, yb, x1, gates, g2, b2)


def _route_meta(top_e, tmb):
    nt = top_e.shape[0]
    nblk = (nt * TOP_K) // tmb + N_EXPERTS
    e_ids = jnp.arange(N_EXPERTS, dtype=I32)
    sel = top_e[:, :, None] == e_ids[None, None, :]
    onehot = sel.sum(axis=1).astype(I32)
    incl = jnp.cumsum(onehot, axis=0)
    counts = incl[-1]
    padded = (counts + tmb - 1) // tmb * tmb
    pend = jnp.cumsum(padded)
    pstart = pend - padded
    start = jnp.cumsum(counts) - counts
    dest = jnp.where(sel, (pstart[None, :] + incl - onehot)[:, None, :], 0).sum(axis=-1).astype(I32)
    n_used = (pend[-1] // tmb).astype(I32)
    blk = jnp.arange(nblk, dtype=I32)
    be = jnp.clip(jnp.searchsorted(pend, blk * tmb, side='right'), 0, N_EXPERTS - 1).astype(I32)
    be = jnp.where(blk < n_used, be, be[jnp.maximum(n_used - 1, 0)])
    order = jnp.argsort(top_e.reshape(-1), stable=True).astype(I32)
    tok_sorted = jnp.concatenate([order // TOP_K, jnp.zeros((tmb,), I32)])
    win = jnp.clip(blk * tmb - pstart[be] + start[be], 0, nt * TOP_K)
    row_tok = jax.vmap(lambda o: lax.dynamic_slice(tok_sorted, (o,), (tmb,)))(win).reshape(-1)
    return be, n_used.reshape(1), row_tok * ROW_TILE, dest * ROW_TILE


def kernel(x_prompt, x_sample, ln_in_g, ln_in_b, w_in, lam_q1, lam_k1, lam_q2, lam_k2, subln_g, w_attn_proj, ssm_lam_re, ssm_lam_im, ssm_log_dt, ssm_b_re, ssm_b_im, ssm_c_re, ssm_c_im, ssm_d, w_glu, b_glu, w_ssm_proj, w_out, ln1_g, ln1_b, w_router, b_router, w_gate, b_gate, w_up, b_up, w_down, b_down, ln2_g, ln2_b):
    l = 0
    row = lambda v: v.reshape(1, -1).astype(F32)
    lambda_init = 0.8 - 0.6 * math.exp(-0.3 * l)
    lam = (jnp.exp(jnp.sum(lam_q1[l].astype(F32) * lam_k1[l].astype(F32)))
           - jnp.exp(jnp.sum(lam_q2[l].astype(F32) * lam_k2[l].astype(F32))) + lambda_init).reshape(1, 1)
    w_in_b = w_in[l].astype(BF16)
    wglu_b = w_glu[l].astype(BF16)
    wap_b = w_attn_proj[l].astype(BF16)
    wsp_b = w_ssm_proj[l].astype(BF16)
    wout_b = w_out[l].astype(BF16)
    wg_b = w_gate[l].astype(BF16)
    wu_b = w_up[l].astype(BF16)
    wd_b = w_down[l].astype(BF16)
    wr = w_router[l].astype(F32)
    wr_hi = wr.astype(BF16)
    wr_split = jnp.concatenate([wr_hi, (wr - wr_hi.astype(F32)).astype(BF16)], axis=1)
    s5_ops = _s5_operators(ssm_lam_re[l], ssm_lam_im[l], ssm_log_dt[l], ssm_b_re[l], ssm_b_im[l],
                           ssm_c_re[l], ssm_c_im[l])
    d_tiled = jnp.tile(ssm_d[l].astype(F32).reshape(SSM_GROUPS, 1, SSM_GROUP), (1, 1, S5_CHUNK))

    def pre_moe(x3):
        batch, seq, _ = x3.shape
        x = x3.reshape(batch * seq, D_MODEL)
        proj = _inproj(x, row(ln_in_g), row(ln_in_b), w_in_b)
        cos_t, sin_t = _rope_tables(seq)
        attn = _attention(proj, lam, cos_t, sin_t, row(subln_g[l]), batch, seq, lambda_init)
        u = proj[:, 3 * ATTN_W:3 * ATTN_W + D_SSM]
        yact = _s5_branch(u, s5_ops, d_tiled, batch, seq)
        merged = _merge(yact, attn, proj, wglu_b, row(b_glu[l]), wap_b, wsp_b)
        return _outproj(x, merged, wout_b, row(ln_in_g), row(ln_in_b), row(ln1_g[l]), row(ln1_b[l]),
                        wr_split, row(b_router[l]))

    parts = [pre_moe(x_prompt), pre_moe(x_sample)]
    xp_all = jnp.concatenate([p[1] for p in parts], axis=0)
    te_all = jnp.concatenate([p[2] for p in parts], axis=0)
    nt = xp_all.shape[0]
    tmb = 1024 if nt >= 8192 else 128
    be, n_used, row_tok, dest = _route_meta(te_all, tmb)
    yb = _moe_experts(be, n_used, row_tok, xp_all, wg_b, b_gate[l].astype(F32)[:, None, :],
                      wu_b, b_up[l].astype(F32)[:, None, :], wd_b, b_down[l].astype(F32)[:, None, :], tmb)

    outs = []
    off = 0
    for x3, p in zip((x_prompt, x_sample), parts):
        n = p[0].shape[0]
        tc = min(256, n)
        d = dest[off:off + n].reshape(n // tc, tc, TOP_K)
        d = jnp.transpose(d, (0, 2, 1)).reshape(-1)
        out = _combine(d, yb, p[0], p[3], row(ln2_g[l]), row(ln2_b[l]), tc)
        outs.append(out.reshape(x3.shape))
        off += n
    return tuple(outs)
```

```python
import functools
import math

import jax
import jax.numpy as jnp
from jax import lax
from jax.experimental import pallas as pl
from jax.experimental.pallas import tpu as pltpu

F32 = jnp.float32
BF16 = jnp.bfloat16
U32 = jnp.uint32
I32 = jnp.int32

D_MODEL = 2048
DEPTH = 1
N_HEADS = 8
HEAD_DIM = 64
HEAD_W = 2 * HEAD_DIM
ATTN_W = N_HEADS * HEAD_W
ROT_DIM = HEAD_DIM // 4
ROPE_THETA = 500000.0
SSM_GROUP = 16
SSM_GROUPS = 64
D_SSM = SSM_GROUP * SSM_GROUPS
SSM_STATE = 64
S5_CHUNK = 16
S5_W = S5_CHUNK * SSM_GROUP
N_IN = 3 * ATTN_W + D_SSM + 2 * D_MODEL
N_EXPERTS = 32
TOP_K = 4
D_FF = D_MODEL
SWIGLU_LIMIT = 7.0
SWIGLU_ALPHA = 1.702
DN_ALPHA = (2.0 * DEPTH) ** 0.25
LN_EPS = 1e-5
HALF = D_MODEL // 2
LANES = 128
ROW_TILE = 8
MOE_TILE = 512

VMEM_LIMIT = 56 * 1024 * 1024


def _cparams(sem, **kw):
    return pltpu.CompilerParams(dimension_semantics=sem, vmem_limit_bytes=VMEM_LIMIT, **kw)


def _ln(x, g, b):
    mu = jnp.mean(x, axis=-1, keepdims=True)
    xc = x - mu
    var = jnp.mean(xc * xc, axis=-1, keepdims=True)
    return xc * lax.rsqrt(var + LN_EPS) * g + b


def _sigmoid(x):
    return 1.0 / (1.0 + jnp.exp(-x))


def _gelu_tanh(x):
    return 0.5 * x * (1.0 + jnp.tanh(math.sqrt(2.0 / math.pi) * (x + 0.044715 * (x * x * x))))


def _pack_words(lo, hi):
    lo = lax.bitcast_convert_type(lo.astype(BF16).astype(F32), U32)
    hi = lax.bitcast_convert_type(hi.astype(BF16).astype(F32), U32)
    return (hi & jnp.uint32(0xFFFF0000)) | (lo >> 16)


def _pack_rows(x):
    return _pack_words(x[:, :HALF], x[:, HALF:])


def _packed_chunks(s, tile):
    sub_per_tile = tile // (2 * LANES)
    n, q = divmod(s, sub_per_tile)
    lo = n * (tile // LANES) + q
    return lo, lo + sub_per_tile


def _unpack_rows(w):
    lo = lax.bitcast_convert_type(w << 16, F32)
    hi = lax.bitcast_convert_type(w & jnp.uint32(0xFFFF0000), F32)
    return lo, hi


def _store_row_tiles(ref, r0, nrows, packed):
    for s in range(ROW_TILE):
        ref[pl.ds(r0 * ROW_TILE + s, nrows, stride=ROW_TILE), :] = packed[:, s * LANES:(s + 1) * LANES]


def _load_row_tile_words(ref, r0, nrows, s):
    return ref[pl.ds(r0 * ROW_TILE + s, nrows, stride=ROW_TILE), :]


def _inproj_kernel(x_ref, g_ref, b_ref, w_ref, o_ref, xn_ref, *, tm, rc):
    @pl.when(pl.program_id(1) == 0)
    def _():
        def body(c, carry):
            r = pl.ds(pl.multiple_of(c * rc, rc), rc)
            xn_ref[r, :] = _ln(x_ref[r, :], g_ref[...], b_ref[...]).astype(BF16)
            return carry
        lax.fori_loop(0, tm // rc, body, 0)

    o_ref[...] = jnp.dot(xn_ref[...], w_ref[...], preferred_element_type=F32).astype(BF16)


def _inproj(x, g, b, w_bf16):
    n = x.shape[0]
    tm = min(1024, n)
    tn = 512
    rc = min(256, tm)
    return pl.pallas_call(
        functools.partial(_inproj_kernel, tm=tm, rc=rc),
        grid=(n // tm, N_IN // tn),
        in_specs=[
            pl.BlockSpec((tm, D_MODEL), lambda i, j: (i, 0)),
            pl.BlockSpec((1, D_MODEL), lambda i, j: (0, 0)),
            pl.BlockSpec((1, D_MODEL), lambda i, j: (0, 0)),
            pl.BlockSpec((D_MODEL, tn), lambda i, j: (0, j)),
        ],
        out_specs=pl.BlockSpec((tm, tn), lambda i, j: (i, j)),
        out_shape=jax.ShapeDtypeStruct((n, N_IN), BF16),
        scratch_shapes=[pltpu.VMEM((tm, D_MODEL), BF16)],
        compiler_params=_cparams(("parallel", "arbitrary")),
        name="ln_inproj",
    )(x, g, b, w_bf16)


def _rope(x, c, s):
    lane = lax.broadcasted_iota(I32, (1, HEAD_W), 1)
    first_half = (lane % HEAD_DIM) < (ROT_DIM // 2)
    xs = jnp.where(first_half, pltpu.roll(x, HEAD_W - ROT_DIM // 2, 1), pltpu.roll(x, ROT_DIM // 2, 1))
    return x * c + xs * s


def _attn_kernel(lam_ref, q_ref, k_ref, v_ref, cq_ref, sq_ref, ck_ref, sk_ref, g_ref, o_ref,
                 k_scr, vt_scr, q_scr, m_scr, l_scr, acc_scr, *, seq, tq, tk, out_scale):
    @pl.when(pl.program_id(2) == 0)
    def _():
        def body(c, carry):
            r = pl.ds(pl.multiple_of(c * tk, tk), tk)
            k_scr[r, :] = _rope(k_ref[r, :].astype(F32), ck_ref[r, :], sk_ref[r, :]).astype(BF16)
            vt_scr[c] = v_ref[r, :].astype(F32).T.astype(BF16)
            return carry
        lax.fori_loop(0, seq // tk, body, 0)

    q = _rope(q_ref[...].astype(F32), cq_ref[...], sq_ref[...]) * (HEAD_DIM ** -0.5 * math.log2(math.e))
    lane = lax.broadcasted_iota(I32, (1, HEAD_W), 1)
    comp1 = lane < HEAD_DIM
    q_scr[pl.ds(0, tq), :] = jnp.where(comp1, q, 0.0).astype(BF16)
    q_scr[pl.ds(tq, tq), :] = jnp.where(comp1, 0.0, q).astype(BF16)
    m_scr[...] = jnp.full(m_scr.shape, -jnp.inf, F32)
    l_scr[...] = jnp.zeros(l_scr.shape, F32)
    acc_scr[...] = jnp.zeros(acc_scr.shape, F32)

    def body(j, carry):
        r = pl.ds(pl.multiple_of(j * tk, tk), tk)
        st = lax.dot_general(k_scr[r, :], q_scr[...], (((1,), (1,)), ((), ())),
                             preferred_element_type=F32)
        m_old = m_scr[...]
        m_new = jnp.maximum(m_old, jnp.max(st, axis=0, keepdims=True))
        p = jnp.exp2(st - m_new)
        a = jnp.exp2(m_old - m_new)
        l_scr[...] = a * l_scr[...] + jnp.sum(p, axis=0, keepdims=True)
        acc_scr[...] = a * acc_scr[...] + jnp.dot(vt_scr[j], p.astype(BF16), preferred_element_type=F32)
        m_scr[...] = m_new
        return carry
    lax.fori_loop(0, seq // tk, body, 0)

    o = acc_scr[...] / l_scr[...]
    o = o[:, :tq] - lam_ref[0, 0] * o[:, tq:]
    o = o * lax.rsqrt(jnp.mean(o * o, axis=0, keepdims=True) + LN_EPS) * g_ref[...] * out_scale
    o_ref[...] = o.T.astype(BF16)


def _attention(proj, lam, cos_t, sin_t, subln_g, batch, seq, lambda_init):
    tq = min(256, seq)
    tk = min(1024, seq)
    nq = seq // tq
    kern = functools.partial(_attn_kernel, seq=seq, tq=tq, tk=tk, out_scale=1.0 - lambda_init)
    return pl.pallas_call(
        kern,
        grid=(batch, N_HEADS, nq),
        in_specs=[
            pl.BlockSpec(memory_space=pltpu.SMEM),
            pl.BlockSpec((tq, HEAD_W), lambda b, h, i: (b * nq + i, h)),
            pl.BlockSpec((seq, HEAD_W), lambda b, h, i: (b, N_HEADS + h)),
            pl.BlockSpec((seq, HEAD_W), lambda b, h, i: (b, 2 * N_HEADS + h)),
            pl.BlockSpec((tq, HEAD_W), lambda b, h, i: (i, 0)),
            pl.BlockSpec((tq, HEAD_W), lambda b, h, i: (i, 0)),
            pl.BlockSpec((seq, HEAD_W), lambda b, h, i: (0, 0)),
            pl.BlockSpec((seq, HEAD_W), lambda b, h, i: (0, 0)),
            pl.BlockSpec((HEAD_W, 1), lambda b, h, i: (0, 0)),
        ],
        out_specs=pl.BlockSpec((tq, HEAD_W), lambda b, h, i: (b * nq + i, h)),
        out_shape=jax.ShapeDtypeStruct((batch * seq, ATTN_W), BF16),
        scratch_shapes=[
            pltpu.VMEM((seq, HEAD_W), BF16),
            pltpu.VMEM((seq // tk, HEAD_W, tk), BF16),
            pltpu.VMEM((2 * tq, HEAD_W), BF16),
            pltpu.VMEM((1, 2 * tq), F32),
            pltpu.VMEM((1, 2 * tq), F32),
            pltpu.VMEM((HEAD_W, 2 * tq), F32),
        ],
        compiler_params=_cparams(("parallel", "parallel", "arbitrary")),
        name="diff_attention",
    )(lam, proj, proj, proj, cos_t, sin_t, cos_t, sin_t, subln_g.reshape(HEAD_W, 1))


def _rope_tables(seq):
    half = ROT_DIM // 2
    inv = ROPE_THETA ** (-jnp.arange(0, ROT_DIM, 2, dtype=F32) / ROT_DIM)
    ang = jnp.arange(seq, dtype=F32)[:, None] * inv[None, :]
    lane = jnp.arange(HEAD_W) % HEAD_DIM
    cos_l = jnp.cos(ang)[:, lane % half]
    sin_l = jnp.sin(ang)[:, lane % half]
    cos_t = jnp.where(lane[None, :] < ROT_DIM, cos_l, 1.0)
    sin_t = jnp.where(lane[None, :] < half, -sin_l, jnp.where(lane[None, :] < ROT_DIM, sin_l, 0.0))
    return cos_t.astype(F32), sin_t.astype(F32)


def _s5_operators(lam_re, lam_im, log_dt, b_re, b_im, c_re, c_im):
    L, G, P, CG = S5_CHUNK, SSM_GROUPS, SSM_STATE, SSM_GROUP
    hp = lax.Precision.HIGHEST
    lr = jnp.minimum(lam_re.astype(F32), -1e-4)
    li = lam_im.astype(F32)
    dt = jnp.exp(log_dt.astype(F32))[..., None]
    nn = jnp.arange(L + 1, dtype=F32)[:, None, None, None]
    mag = jnp.exp(nn * (lr * dt)[None])
    pw_r = mag * jnp.cos(nn * (li * dt)[None])
    pw_i = mag * jnp.sin(nn * (li * dt)[None])
    ab_r, ab_i = pw_r[1], pw_i[1]
    nr = ab_r - 1.0
    den = lr * lr + li * li
    cr_ = ((nr * lr + ab_i * li) / den)[..., None]
    ci_ = ((ab_i * lr - nr * li) / den)[..., None]
    br = b_re.astype(F32)
    bi = b_im.astype(F32)
    bb_r = cr_ * br - ci_ * bi
    bb_i = cr_ * bi + ci_ * br
    cr = c_re.astype(F32)
    ci = c_im.astype(F32)

    m_r = pw_r[..., None] * bb_r[None] - pw_i[..., None] * bb_i[None]
    m_i = pw_r[..., None] * bb_i[None] + pw_i[..., None] * bb_r[None]
    kk = (jnp.einsum('dgop,ndgpi->ndgoi', cr, m_r, precision=hp)
          - jnp.einsum('dgop,ndgpi->ndgoi', ci, m_i, precision=hp))
    ii = jnp.arange(L)[:, None]
    jj = jnp.arange(L)[None, :]
    kf = jnp.where((jj >= ii)[:, :, None, None, None], kk[:, 0][jnp.clip(jj - ii, 0, L)], 0.0)
    kb = jnp.where((ii >= jj)[:, :, None, None, None], kk[:, 1][jnp.clip(ii - jj, 0, L)], 0.0)
    t_op = jnp.transpose(kf + kb, (2, 0, 4, 1, 3)).reshape(G, S5_W, S5_W)

    def state_in(mr, mi):
        e_re = jnp.transpose(mr, (1, 0, 3, 2)).reshape(G, S5_W, P)
        e_im = jnp.transpose(mi, (1, 0, 3, 2)).reshape(G, S5_W, P)
        return jnp.concatenate([e_re, e_im, e_im, e_re], axis=-1)
    bm = jnp.concatenate([state_in(m_r[L - 1::-1, 0][:L], m_i[L - 1::-1, 0][:L]),
                          state_in(m_r[:L, 1], m_i[:L, 1])], axis=-1)

    def state_out(d, pows):
        pr = pw_r[pows, d]
        pi = pw_i[pows, d]
        g_r = cr[d][None] * pr[:, :, None, :] - ci[d][None] * pi[:, :, None, :]
        g_i = cr[d][None] * pi[:, :, None, :] + ci[d][None] * pr[:, :, None, :]
        rows_re = jnp.transpose(g_r, (1, 3, 0, 2)).reshape(G, P, S5_W)
        rows_im = jnp.transpose(-g_i, (1, 3, 0, 2)).reshape(G, P, S5_W)
        return jnp.concatenate([rows_re, rows_im], axis=1)
    cm = jnp.concatenate([state_out(0, jnp.arange(1, L + 1)), state_out(1, jnp.arange(L, 0, -1))], axis=1)

    def step_mult(d):
        ar = jnp.concatenate([pw_r[L, d]] * 4, axis=-1)
        ai = jnp.concatenate([-pw_i[L, d], pw_i[L, d], pw_i[L, d], -pw_i[L, d]], axis=-1)
        return ar, ai
    arf, aif = step_mult(0)
    arb, aib = step_mult(1)
    a_r = jnp.concatenate([arf, arb], axis=-1)[:, None, :]
    a_i = jnp.concatenate([aif, aib], axis=-1)[:, None, :]
    return t_op.astype(BF16), bm.astype(BF16), cm.astype(BF16), a_r, a_i


S5_LANE_GROUPS = LANES // SSM_GROUP
S5_CAT = S5_CHUNK * LANES


def _s5_kernel(u_ref, t_ref, bm_ref, cm_ref, ar_ref, ai_ref, d_ref, o_ref,
               perm_scr, stage_scr, uall_scr, e_scr, s_scr, *, nc, nb, seg):
    rows = nb * nc
    w2 = S5_W
    p2 = 2 * SSM_STATE

    @pl.when((pl.program_id(0) == 0) & (pl.program_id(1) == 0))
    def _():
        def body(t, carry):
            gc = lax.broadcasted_iota(I32, (LANES, S5_CAT), 0)
            tgt = (gc // SSM_GROUP) * S5_W + t * SSM_GROUP + gc % SSM_GROUP
            col = lax.broadcasted_iota(I32, (LANES, S5_CAT), 1)
            r0 = pl.multiple_of(t * LANES, LANES)
            perm_scr[pl.ds(r0, LANES), :] = jnp.where(col == tgt, 1.0, 0.0).astype(BF16)
            return carry
        lax.fori_loop(0, S5_CHUNK, body, 0)

    def relayout_in(k, carry):
        tok0 = pl.multiple_of(k * (seg * S5_CHUNK), seg * S5_CHUNK)
        stage_scr[...] = u_ref[pl.ds(tok0, seg * S5_CHUNK), :].astype(F32)
        z = jnp.concatenate([stage_scr[pl.ds(t, seg, stride=S5_CHUNK), :] for t in range(S5_CHUNK)], axis=1)
        r0 = pl.multiple_of(k * seg, seg)
        uall_scr[pl.ds(r0, seg), :] = jnp.dot(z.astype(BF16), perm_scr[...],
                                              preferred_element_type=F32).astype(BF16)
        return carry
    lax.fori_loop(0, rows // seg, relayout_in, 0)

    for gp in range(S5_LANE_GROUPS // 2):
        pair = (2 * gp, 2 * gp + 1)
        for q, g in enumerate(pair):
            e = jnp.dot(uall_scr[:, g * w2:(g + 1) * w2], bm_ref[g], preferred_element_type=F32)
            for j in range(4):
                e_scr[4 * q + j] = e[:, j * p2:(j + 1) * p2]
        mult = [[(ar_ref[g, :, j * p2:(j + 1) * p2], ai_ref[g, :, j * p2:(j + 1) * p2]) for j in range(4)]
                for g in pair]

        def body(c, carry):
            rf = pl.ds(c, nb, stride=nc)
            rb = pl.ds(nc - 1 - c, nb, stride=nc)
            out = []
            for q in range(2):
                xf, xfs, zb, zbs = carry[4 * q:4 * q + 4]
                m = mult[q]
                s_scr[2 * q, rf, :] = xf
                s_scr[2 * q + 1, rb, :] = zb
                out += [m[0][0] * xf + m[0][1] * xfs + e_scr[4 * q, rf, :],
                        m[1][0] * xfs + m[1][1] * xf + e_scr[4 * q + 1, rf, :],
                        m[2][0] * zb + m[2][1] * zbs + e_scr[4 * q + 2, rb, :],
                        m[3][0] * zbs + m[3][1] * zb + e_scr[4 * q + 3, rb, :]]
            return tuple(out)
        z0 = jnp.zeros((nb, p2), F32)
        lax.fori_loop(0, nc, body, (z0,) * 8)

        for q, g in enumerate(pair):
            ug = uall_scr[:, g * w2:(g + 1) * w2]
            st = jnp.concatenate([s_scr[2 * q], s_scr[2 * q + 1]], axis=1).astype(BF16)
            y = (jnp.dot(ug, t_ref[g], preferred_element_type=F32)
                 + jnp.dot(st, cm_ref[g], preferred_element_type=F32)
                 + d_ref[g] * ug.astype(F32))
            uall_scr[:, g * w2:(g + 1) * w2] = _gelu_tanh(y).astype(BF16)

    def relayout_out(k, carry):
        r0 = pl.multiple_of(k * seg, seg)
        yt = lax.dot_general(uall_scr[pl.ds(r0, seg), :], perm_scr[...], (((1,), (1,)), ((), ())),
                             preferred_element_type=F32)
        for t in range(S5_CHUNK):
            stage_scr[pl.ds(t, seg, stride=S5_CHUNK), :] = yt[:, t * LANES:(t + 1) * LANES]
        tok0 = pl.multiple_of(k * (seg * S5_CHUNK), seg * S5_CHUNK)
        o_ref[pl.ds(tok0, seg * S5_CHUNK), :] = stage_scr[...].astype(BF16)
        return carry
    lax.fori_loop(0, rows // seg, relayout_out, 0)


def _s5_branch(proj, ops, d_tiled, batch, seq):
    t_op, bm, cm, a_r, a_i = ops
    n = batch * seq
    nc = seq // S5_CHUNK
    rows_blk = min(16384, n)
    nb = rows_blk // seq
    seg = min(128, nb * nc)
    ucol = (3 * ATTN_W) // LANES
    lg = S5_LANE_GROUPS
    blk = lambda gb, sb: (gb, 0, 0)
    return pl.pallas_call(
        functools.partial(_s5_kernel, nc=nc, nb=nb, seg=seg),
        grid=(D_SSM // LANES, n // rows_blk),
        in_specs=[
            pl.BlockSpec((rows_blk, LANES), lambda gb, sb: (sb, ucol + gb)),
            pl.BlockSpec((lg, S5_W, S5_W), blk),
            pl.BlockSpec((lg, S5_W, 2 * S5_W), blk),
            pl.BlockSpec((lg, S5_W, S5_W), blk),
            pl.BlockSpec((lg, 1, 2 * S5_W), blk),
            pl.BlockSpec((lg, 1, 2 * S5_W), blk),
            pl.BlockSpec((lg, 1, S5_W), blk),
        ],
        out_specs=pl.BlockSpec((rows_blk, LANES), lambda gb, sb: (sb, gb)),
        out_shape=jax.ShapeDtypeStruct((n, D_SSM), BF16),
        scratch_shapes=[
            pltpu.VMEM((S5_CAT, S5_CAT), BF16),
            pltpu.VMEM((seg * S5_CHUNK, LANES), F32),
            pltpu.VMEM((nb * nc, S5_CAT), BF16),
            pltpu.VMEM((8, nb * nc, LANES), F32),
            pltpu.VMEM((4, nb * nc, LANES), F32),
        ],
        compiler_params=_cparams(("arbitrary", "arbitrary")),
        name="s5_chunked",
    )(proj, t_op, bm, cm, a_r, a_i, d_tiled)


def _merge_kernel(y_ref, a_ref, gs_ref, ga_ref, wglu_ref, bglu_ref, wap_ref, wsp_ref, o_ref):
    y = y_ref[...]
    z = jnp.dot(y, wglu_ref[...], preferred_element_type=F32) + bglu_ref[...]
    act = (y.astype(F32) * _sigmoid(z)).astype(BF16)
    pa = jnp.dot(a_ref[...], wap_ref[...], preferred_element_type=F32)
    ps = jnp.dot(act, wsp_ref[...], preferred_element_type=F32)
    o = _sigmoid(ga_ref[...].astype(F32)) * pa + _sigmoid(gs_ref[...].astype(F32)) * ps
    o_ref[...] = o.astype(BF16)


def _merge(yact, attn, proj, wglu, bglu, wap, wsp):
    n = yact.shape[0]
    tm = min(256, n)
    gcol = (3 * ATTN_W + D_SSM) // D_MODEL
    const = lambda i: (0, 0)
    return pl.pallas_call(
        _merge_kernel,
        grid=(n // tm,),
        in_specs=[
            pl.BlockSpec((tm, D_SSM), lambda i: (i, 0)),
            pl.BlockSpec((tm, ATTN_W), lambda i: (i, 0)),
            pl.BlockSpec((tm, D_MODEL), lambda i: (i, gcol)),
            pl.BlockSpec((tm, D_MODEL), lambda i: (i, gcol + 1)),
            pl.BlockSpec((D_SSM, D_SSM), const),
            pl.BlockSpec((1, D_SSM), const),
            pl.BlockSpec((ATTN_W, D_MODEL), const),
            pl.BlockSpec((D_SSM, D_MODEL), const),
        ],
        out_specs=pl.BlockSpec((tm, D_MODEL), lambda i: (i, 0)),
        out_shape=jax.ShapeDtypeStruct((n, D_MODEL), BF16),
        compiler_params=_cparams(("parallel",)),
        name="glu_merge",
    )(yact, attn, proj, proj, wglu, bglu, wap, wsp)


def _outproj_kernel(x_ref, m_ref, wout_ref, gin_ref, bin_ref, g1_ref, b1_ref, wr_ref, br_ref,
                    x1_ref, xp_ref, te_ref, tg_ref):
    xn = _ln(x_ref[...], gin_ref[...], bin_ref[...])
    z = DN_ALPHA * xn + jnp.dot(m_ref[...], wout_ref[...], preferred_element_type=F32)
    x1 = _ln(z, g1_ref[...], b1_ref[...])
    x1_ref[...] = x1
    _store_row_tiles(xp_ref, 0, x1.shape[0], _pack_rows(x1))

    x_hi = x1.astype(BF16)
    x_lo = (x1 - x_hi.astype(F32)).astype(BF16)
    r = (jnp.dot(x_hi, wr_ref[...], preferred_element_type=F32)
         + jnp.dot(x_lo, wr_ref[...], preferred_element_type=F32))
    logits = r[:, :N_EXPERTS] + r[:, N_EXPERTS:] + br_ref[...]
    e_iota = lax.broadcasted_iota(I32, logits.shape, 1)
    k_iota = lax.broadcasted_iota(I32, (logits.shape[0], TOP_K), 1)
    vals = jnp.zeros((logits.shape[0], TOP_K), F32)
    idxs = jnp.zeros((logits.shape[0], TOP_K), I32)
    cur = logits
    for k in range(TOP_K):
        m = jnp.max(cur, axis=-1, keepdims=True)
        idx = jnp.min(jnp.where(cur == m, e_iota, N_EXPERTS), axis=-1, keepdims=True)
        vals = jnp.where(k_iota == k, m, vals)
        idxs = jnp.where(k_iota == k, idx, idxs)
        cur = jnp.where(e_iota == idx, -jnp.inf, cur)
    ex = jnp.exp(vals - jnp.max(vals, axis=-1, keepdims=True))
    te_ref[...] = idxs
    tg_ref[...] = ex / jnp.sum(ex, axis=-1, keepdims=True)


def _outproj(x, merged, wout, gin, bin_, g1, b1, wr, br):
    n = x.shape[0]
    tm = min(256, n)
    const = lambda i: (0, 0)
    row = lambda i: (i, 0)
    return pl.pallas_call(
        _outproj_kernel,
        grid=(n // tm,),
        in_specs=[
            pl.BlockSpec((tm, D_MODEL), row),
            pl.BlockSpec((tm, D_MODEL), row),
            pl.BlockSpec((D_MODEL, D_MODEL), const),
            pl.BlockSpec((1, D_MODEL), const),
            pl.BlockSpec((1, D_MODEL), const),
            pl.BlockSpec((1, D_MODEL), const),
            pl.BlockSpec((1, D_MODEL), const),
            pl.BlockSpec((D_MODEL, 2 * N_EXPERTS), const),
            pl.BlockSpec((1, N_EXPERTS), const),
        ],
        out_specs=[
            pl.BlockSpec((tm, D_MODEL), row),
            pl.BlockSpec((tm * ROW_TILE, LANES), row),
            pl.BlockSpec((tm, TOP_K), row),
            pl.BlockSpec((tm, TOP_K), row),
        ],
        out_shape=[
            jax.ShapeDtypeStruct((n, D_MODEL), F32),
            jax.ShapeDtypeStruct((n * ROW_TILE, LANES), U32),
            jax.ShapeDtypeStruct((n, TOP_K), I32),
            jax.ShapeDtypeStruct((n, TOP_K), F32),
        ],
        compiler_params=_cparams(("parallel",)),
        name="outproj_ln_router",
    )(x, merged, wout, gin, bin_, g1, b1, wr, br)


def _gather_rows(idx_smem, base, src_hbm, dst_vmem, sem, count):
    def body(i, carry):
        src_row = pl.multiple_of(idx_smem[base + i], ROW_TILE)
        dst_row = pl.multiple_of(i * ROW_TILE, ROW_TILE)
        pltpu.make_async_copy(src_hbm.at[pl.ds(src_row, ROW_TILE)], dst_vmem.at[pl.ds(dst_row, ROW_TILE)],
                              sem).start()
        return carry
    lax.fori_loop(0, count, body, 0, unroll=8)


def _wait_rows(src_hbm, dst_vmem, sem, count):
    pltpu.make_async_copy(src_hbm.at[pl.ds(0, count * ROW_TILE)], dst_vmem, sem).wait()


def _prefetch_gather(step, nsteps, idx_hbm, idx_smem, idx_sem, src_hbm, gbuf, gsem, count):
    slot = step % 2
    nxt = 1 - slot

    def idx_copy(s, sl):
        return pltpu.make_async_copy(idx_hbm.at[pl.ds(pl.multiple_of(s * count, count), count)],
                                     idx_smem.at[pl.ds(pl.multiple_of(sl * count, count), count)],
                                     idx_sem.at[sl])

    @pl.when(step == 0)
    def _():
        c = idx_copy(0, 0)
        c.start()
        c.wait()
        _gather_rows(idx_smem, 0, src_hbm, gbuf.at[0], gsem.at[0], count)

        @pl.when(nsteps > 1)
        def _():
            idx_copy(1, 1).start()

    _wait_rows(src_hbm, gbuf.at[slot], gsem.at[slot], count)

    @pl.when(step + 1 < nsteps)
    def _():
        idx_copy(step + 1, nxt).wait()
        _gather_rows(idx_smem, nxt * count, src_hbm, gbuf.at[nxt], gsem.at[nxt], count)

    @pl.when(step + 2 < nsteps)
    def _():
        idx_copy(step + 2, slot).start()


def _moe_kernel(be_ref, nu_ref, sh_ref, idx_hbm, x_hbm, wg_ref, bg_ref, wu_ref, bu_ref, wd_ref, bd_ref, o_ref,
                idx_smem, idx_sem, gbuf, gsem, xb_scr, h_scr, *, tmb, rc, nf, nn, tf, tn):
    b = pl.program_id(0)
    t = pl.program_id(1)
    n_used = nu_ref[0]
    active = b < n_used
    slot = b % 2
    nxt = 1 - slot
    per = tmb // (nf + nn)
    win = 2 * tmb

    def idx_copy(s, sl):
        return pltpu.make_async_copy(idx_hbm.at[pl.ds(pl.multiple_of(s * win, win), win)],
                                     idx_smem.at[pl.ds(pl.multiple_of(sl * win, win), win)],
                                     idx_sem.at[sl])

    def gather_next_slice():
        base = nxt * win + sh_ref[b + 1] + t * per
        for i in range(per):
            src_row = pl.multiple_of(idx_smem[base + i], ROW_TILE)
            dst_row = pl.multiple_of((t * per + i) * ROW_TILE, ROW_TILE)
            pltpu.make_async_copy(x_hbm.at[pl.ds(src_row, ROW_TILE)], gbuf.at[nxt, pl.ds(dst_row, ROW_TILE)],
                                  gsem.at[nxt]).start()

    @pl.when((b <= n_used) & (t == 0))
    def _():
        @pl.when(b == 0)
        def _():
            c = idx_copy(0, 0)
            c.start()
            c.wait()
            _gather_rows(idx_smem, sh_ref[0], x_hbm, gbuf.at[0], gsem.at[0], tmb)
            idx_copy(1, 1).start()

        _wait_rows(x_hbm, gbuf.at[slot], gsem.at[slot], tmb)

    @pl.when(active & (t == 0))
    def _():
        idx_copy(b + 1, nxt).wait()

        @pl.when(b + 2 <= n_used)
        def _():
            idx_copy(b + 2, slot).start()

        def body(c, carry):
            r0 = pl.multiple_of(c * rc, rc)
            r = pl.ds(r0, rc)
            for s in range(ROW_TILE):
                c_lo, c_hi = _packed_chunks(s, D_MODEL)
                lo, hi = _unpack_rows(_load_row_tile_words(gbuf.at[slot], r0, rc, s))
                xb_scr[r, c_lo * LANES:(c_lo + 1) * LANES] = lo.astype(BF16)
                xb_scr[r, c_hi * LANES:(c_hi + 1) * LANES] = hi.astype(BF16)
            return carry
        lax.fori_loop(0, tmb // rc, body, 0)

    @pl.when(active & (t < nf))
    def _():
        gather_next_slice()
        x = xb_scr[...]
        g = jnp.dot(x, wg_ref[0], preferred_element_type=F32) + bg_ref[0]
        u = jnp.dot(x, wu_ref[0], preferred_element_type=F32) + bu_ref[0]
        g = jnp.minimum(g, SWIGLU_LIMIT)
        u = jnp.clip(u, -SWIGLU_LIMIT, SWIGLU_LIMIT)
        h = (g * _sigmoid(SWIGLU_ALPHA * g) * (u + 1.0)).astype(BF16)
        for f in range(nf):
            @pl.when(t == f)
            def _():
                h_scr[:, f * tf:(f + 1) * tf] = h

    @pl.when(active & (t >= nf))
    def _():
        gather_next_slice()
        y = jnp.dot(h_scr[...], wd_ref[0], preferred_element_type=F32) + bd_ref[0]
        sub_per_tile = tn // (2 * LANES)
        s0 = (t - nf) * sub_per_tile
        for c in range(tmb // rc):
            yc = y[c * rc:(c + 1) * rc]
            for q in range(sub_per_tile):
                lo = yc[:, q * LANES:(q + 1) * LANES]
                hi = yc[:, tn // 2 + q * LANES:tn // 2 + (q + 1) * LANES]
                o_ref[pl.ds(c * rc * ROW_TILE + s0 + q, rc, stride=ROW_TILE), :] = _pack_words(lo, hi)

    @pl.when(jnp.logical_not(active) & (t == nf + nn - 1))
    def _():
        o_ref[...] = jnp.zeros(o_ref.shape, U32)


def _moe_experts(block_e, n_used, shift, row_win, x_packed, wg, bg, wu, bu, wd, bd, tmb):
    nblk = block_e.shape[0]
    tf = MOE_TILE
    tn = MOE_TILE
    nf = D_FF // tf
    nn = D_MODEL // tn
    rc = min(256, tmb)

    def fsel(b, t, nu):
        return jnp.where(b < nu[0], jnp.minimum(t, nf - 1), nf - 1)

    def nsel(b, t, nu):
        return jnp.where(b < nu[0], jnp.maximum(t - nf, 0), nn - 1)

    grid_spec = pltpu.PrefetchScalarGridSpec(
        num_scalar_prefetch=3,
        grid=(nblk, nf + nn),
        in_specs=[
            pl.BlockSpec(memory_space=pl.ANY),
            pl.BlockSpec(memory_space=pl.ANY),
            pl.BlockSpec((1, D_MODEL, tf), lambda b, t, be, nu, sh: (be[b], 0, fsel(b, t, nu))),
            pl.BlockSpec((1, 1, tf), lambda b, t, be, nu, sh: (be[b], 0, fsel(b, t, nu))),
            pl.BlockSpec((1, D_MODEL, tf), lambda b, t, be, nu, sh: (be[b], 0, fsel(b, t, nu))),
            pl.BlockSpec((1, 1, tf), lambda b, t, be, nu, sh: (be[b], 0, fsel(b, t, nu))),
            pl.BlockSpec((1, D_FF, tn), lambda b, t, be, nu, sh: (be[b], 0, nsel(b, t, nu))),
            pl.BlockSpec((1, 1, tn), lambda b, t, be, nu, sh: (be[b], 0, nsel(b, t, nu))),
        ],
        out_specs=pl.BlockSpec((tmb * ROW_TILE, LANES), lambda b, t, be, nu, sh: (b, 0)),
        scratch_shapes=[
            pltpu.SMEM((4 * tmb,), I32),
            pltpu.SemaphoreType.DMA((2,)),
            pltpu.VMEM((2, tmb * ROW_TILE, LANES), U32),
            pltpu.SemaphoreType.DMA((2,)),
            pltpu.VMEM((tmb, D_MODEL), BF16),
            pltpu.VMEM((tmb, D_FF), BF16),
        ],
    )
    return pl.pallas_call(
        functools.partial(_moe_kernel, tmb=tmb, rc=rc, nf=nf, nn=nn, tf=tf, tn=tn),
        grid_spec=grid_spec,
        out_shape=jax.ShapeDtypeStruct((nblk * tmb * ROW_TILE, LANES), U32),
        compiler_params=_cparams(("arbitrary", "arbitrary"), disable_bounds_checks=True),
        name="moe_experts",
    )(block_e, n_used, shift, row_win, x_packed, wg, bg, wu, bu, wd, bd)


def _combine_kernel(idx_hbm, y_hbm, x1_ref, gate_ref, g_ref, b_ref, o_ref,
                    idx_smem, idx_sem, gbuf, gsem, *, tc):
    i = pl.program_id(0)
    _prefetch_gather(i, pl.num_programs(0), idx_hbm, idx_smem, idx_sem, y_hbm, gbuf, gsem, TOP_K * tc)
    slot = i % 2
    gates = gate_ref[...]
    gk = [gates[:, k:k + 1] for k in range(TOP_K)]
    ncol = D_MODEL // LANES
    z = [DN_ALPHA * x1_ref[:, c * LANES:(c + 1) * LANES] for c in range(ncol)]
    for s in range(ROW_TILE):
        c_lo, c_hi = _packed_chunks(s, MOE_TILE)
        for k in range(TOP_K):
            lo, hi = _unpack_rows(_load_row_tile_words(gbuf.at[slot], k * tc, tc, s))
            z[c_lo] = z[c_lo] + gk[k] * lo
            z[c_hi] = z[c_hi] + gk[k] * hi
    mu = sum(jnp.sum(zc, axis=-1, keepdims=True) for zc in z) / D_MODEL
    z = [zc - mu for zc in z]
    var = sum(jnp.sum(zc * zc, axis=-1, keepdims=True) for zc in z) / D_MODEL
    inv = lax.rsqrt(var + LN_EPS)
    for c in range(ncol):
        cols = slice(c * LANES, (c + 1) * LANES)
        o_ref[:, cols] = z[c] * inv * g_ref[:, cols] + b_ref[:, cols]


def _combine(dest_tiles, yb, x1, gates, g2, b2, tc):
    n = x1.shape[0]
    const = lambda i: (0, 0)
    row = lambda i: (i, 0)
    return pl.pallas_call(
        functools.partial(_combine_kernel, tc=tc),
        grid=(n // tc,),
        in_specs=[
            pl.BlockSpec(memory_space=pl.ANY),
            pl.BlockSpec(memory_space=pl.ANY),
            pl.BlockSpec((tc, D_MODEL), row),
            pl.BlockSpec((tc, TOP_K), row),
            pl.BlockSpec((1, D_MODEL), const),
            pl.BlockSpec((1, D_MODEL), const),
        ],
        out_specs=pl.BlockSpec((tc, D_MODEL), row),
        out_shape=jax.ShapeDtypeStruct((n, D_MODEL), F32),
        scratch_shapes=[
            pltpu.SMEM((2 * TOP_K * tc,), I32),
            pltpu.SemaphoreType.DMA((2,)),
            pltpu.VMEM((2, TOP_K * tc * ROW_TILE, LANES), U32),
            pltpu.SemaphoreType.DMA((2,)),
        ],
        compiler_params=_cparams(("arbitrary",), disable_bounds_checks=True),
        name="moe_combine_ln",
    )(dest_tiles, yb, x1, gates, g2, b2)


def _route_meta(top_e, tmb):
    nt = top_e.shape[0]
    nblk = (nt * TOP_K) // tmb + N_EXPERTS
    e_ids = jnp.arange(N_EXPERTS, dtype=I32)
    sel = top_e[:, :, None] == e_ids[None, None, :]
    onehot = sel.sum(axis=1).astype(I32)
    incl = jnp.cumsum(onehot, axis=0)
    counts = incl[-1]
    padded = (counts + tmb - 1) // tmb * tmb
    pend = jnp.cumsum(padded)
    pstart = pend - padded
    start = jnp.cumsum(counts) - counts
    dest = jnp.where(sel, (pstart[None, :] + incl - onehot)[:, None, :], 0).sum(axis=-1).astype(I32)
    n_used = (pend[-1] // tmb).astype(I32)
    blk = jnp.arange(nblk, dtype=I32)
    be = jnp.clip(jnp.searchsorted(pend, blk * tmb, side='right'), 0, N_EXPERTS - 1).astype(I32)
    be = jnp.where(blk < n_used, be, be[jnp.maximum(n_used - 1, 0)])
    order = jnp.argsort(top_e.reshape(-1), stable=True).astype(I32)
    lines = (nt * TOP_K) // LANES + 2 * tmb // LANES + 1
    tok_sorted = jnp.zeros((lines * LANES,), I32).at[:nt * TOP_K].set((order // TOP_K) * ROW_TILE)
    win = jnp.clip(blk * tmb - pstart[be] + start[be], 0, nt * TOP_K)
    line_idx = (win // LANES)[:, None] + jnp.arange(2 * tmb // LANES, dtype=I32)[None, :]
    row_win = jnp.take(tok_sorted.reshape(lines, LANES), line_idx, axis=0).reshape(-1)
    shift = (win % LANES).astype(I32)
    return be, n_used.reshape(1), shift, row_win, dest * ROW_TILE


def kernel(x_prompt, x_sample, ln_in_g, ln_in_b, w_in, lam_q1, lam_k1, lam_q2, lam_k2, subln_g, w_attn_proj, ssm_lam_re, ssm_lam_im, ssm_log_dt, ssm_b_re, ssm_b_im, ssm_c_re, ssm_c_im, ssm_d, w_glu, b_glu, w_ssm_proj, w_out, ln1_g, ln1_b, w_router, b_router, w_gate, b_gate, w_up, b_up, w_down, b_down, ln2_g, ln2_b):
    l = 0
    row = lambda v: v.reshape(1, -1).astype(F32)
    lambda_init = 0.8 - 0.6 * math.exp(-0.3 * l)
    lam = (jnp.exp(jnp.sum(lam_q1[l].astype(F32) * lam_k1[l].astype(F32)))
           - jnp.exp(jnp.sum(lam_q2[l].astype(F32) * lam_k2[l].astype(F32))) + lambda_init).reshape(1, 1)
    w_in_b = w_in[l].astype(BF16)
    wglu_b = w_glu[l].astype(BF16)
    wap_b = w_attn_proj[l].astype(BF16)
    wsp_b = w_ssm_proj[l].astype(BF16)
    wout_b = w_out[l].astype(BF16)
    wg_b = w_gate[l].astype(BF16)
    wu_b = w_up[l].astype(BF16)
    wd_b = w_down[l].astype(BF16)
    wr = w_router[l].astype(F32)
    wr_hi = wr.astype(BF16)
    wr_split = jnp.concatenate([wr_hi, (wr - wr_hi.astype(F32)).astype(BF16)], axis=1)
    s5_ops = _s5_operators(ssm_lam_re[l], ssm_lam_im[l], ssm_log_dt[l], ssm_b_re[l], ssm_b_im[l],
                           ssm_c_re[l], ssm_c_im[l])
    d_tiled = jnp.tile(ssm_d[l].astype(F32).reshape(SSM_GROUPS, 1, SSM_GROUP), (1, 1, S5_CHUNK))

    def pre_moe(x3):
        batch, seq, _ = x3.shape
        x = x3.reshape(batch * seq, D_MODEL)
        proj = _inproj(x, row(ln_in_g), row(ln_in_b), w_in_b)
        cos_t, sin_t = _rope_tables(seq)
        attn = _attention(proj, lam, cos_t, sin_t, row(subln_g[l]), batch, seq, lambda_init)
        yact = _s5_branch(proj, s5_ops, d_tiled, batch, seq)
        merged = _merge(yact, attn, proj, wglu_b, row(b_glu[l]), wap_b, wsp_b)
        return _outproj(x, merged, wout_b, row(ln_in_g), row(ln_in_b), row(ln1_g[l]), row(ln1_b[l]),
                        wr_split, row(b_router[l]))

    parts = [pre_moe(x_prompt), pre_moe(x_sample)]
    xp_all = jnp.concatenate([p[1] for p in parts], axis=0)
    te_all = jnp.concatenate([p[2] for p in parts], axis=0)
    nt = xp_all.shape[0]
    tmb = 1024 if nt >= 8192 else 128
    be, n_used, shift, row_win, dest = _route_meta(te_all, tmb)
    yb = _moe_experts(be, n_used, shift, row_win, xp_all, wg_b, b_gate[l].astype(F32)[:, None, :],
                      wu_b, b_up[l].astype(F32)[:, None, :], wd_b, b_down[l].astype(F32)[:, None, :], tmb)

    outs = []
    off = 0
    for x3, p in zip((x_prompt, x_sample), parts):
        n = p[0].shape[0]
        tc = min(256, n)
        d = dest[off:off + n].reshape(n // tc, tc, TOP_K)
        d = jnp.transpose(d, (0, 2, 1)).reshape(-1)
        out = _combine(d, yb, p[0], p[3], row(ln2_g[l]), row(ln2_b[l]), tc)
        outs.append(out.reshape(x3.shape))
        off += n
    return tuple(outs)
```

```python
import functools
import math

import jax
import jax.numpy as jnp
from jax import lax
from jax.experimental import pallas as pl
from jax.experimental.pallas import tpu as pltpu

F32 = jnp.float32
BF16 = jnp.bfloat16
U32 = jnp.uint32
I32 = jnp.int32

D_MODEL = 2048
DEPTH = 1
N_HEADS = 8
HEAD_DIM = 64
HEAD_W = 2 * HEAD_DIM
ATTN_W = N_HEADS * HEAD_W
ROT_DIM = HEAD_DIM // 4
ROPE_THETA = 500000.0
SSM_GROUP = 16
SSM_GROUPS = 64
D_SSM = SSM_GROUP * SSM_GROUPS
SSM_STATE = 64
S5_CHUNK = 16
S5_W = S5_CHUNK * SSM_GROUP
N_IN = 3 * ATTN_W + D_SSM + 2 * D_MODEL
N_EXPERTS = 32
TOP_K = 4
D_FF = D_MODEL
SWIGLU_LIMIT = 7.0
SWIGLU_ALPHA = 1.702
DN_ALPHA = (2.0 * DEPTH) ** 0.25
LN_EPS = 1e-5
HALF = D_MODEL // 2
LANES = 128
ROW_TILE = 8
MOE_TILE = 512

VMEM_LIMIT = 56 * 1024 * 1024


def _cparams(sem, **kw):
    return pltpu.CompilerParams(dimension_semantics=sem, vmem_limit_bytes=VMEM_LIMIT, **kw)


def _ln(x, g, b):
    mu = jnp.mean(x, axis=-1, keepdims=True)
    xc = x - mu
    var = jnp.mean(xc * xc, axis=-1, keepdims=True)
    return xc * lax.rsqrt(var + LN_EPS) * g + b


def _sigmoid(x):
    return 1.0 / (1.0 + jnp.exp(-x))


def _gelu_tanh(x):
    return 0.5 * x * (1.0 + jnp.tanh(math.sqrt(2.0 / math.pi) * (x + 0.044715 * (x * x * x))))


def _pack_words(lo, hi):
    lo = lax.bitcast_convert_type(lo.astype(BF16).astype(F32), U32)
    hi = lax.bitcast_convert_type(hi.astype(BF16).astype(F32), U32)
    return (hi & jnp.uint32(0xFFFF0000)) | (lo >> 16)


def _pack_rows(x):
    return _pack_words(x[:, :HALF], x[:, HALF:])


def _packed_chunks(s, tile):
    sub_per_tile = tile // (2 * LANES)
    n, q = divmod(s, sub_per_tile)
    lo = n * (tile // LANES) + q
    return lo, lo + sub_per_tile


def _unpack_rows(w):
    lo = lax.bitcast_convert_type(w << 16, F32)
    hi = lax.bitcast_convert_type(w & jnp.uint32(0xFFFF0000), F32)
    return lo, hi


def _store_row_tiles(ref, r0, nrows, packed):
    for s in range(ROW_TILE):
        ref[pl.ds(r0 * ROW_TILE + s, nrows, stride=ROW_TILE), :] = packed[:, s * LANES:(s + 1) * LANES]


def _load_row_tile_words(ref, r0, nrows, s):
    return ref[pl.ds(r0 * ROW_TILE + s, nrows, stride=ROW_TILE), :]


def _inproj_kernel(x_ref, g_ref, b_ref, w_ref, o_ref, xn_ref, *, tm, rc):
    @pl.when(pl.program_id(1) == 0)
    def _():
        def body(c, carry):
            r = pl.ds(pl.multiple_of(c * rc, rc), rc)
            xn_ref[r, :] = _ln(x_ref[r, :], g_ref[...], b_ref[...]).astype(BF16)
            return carry
        lax.fori_loop(0, tm // rc, body, 0)

    o_ref[...] = jnp.dot(xn_ref[...], w_ref[...], preferred_element_type=F32).astype(BF16)


def _inproj(x, g, b, w_bf16):
    n = x.shape[0]
    tm = min(1024, n)
    tn = 1024
    rc = min(256, tm)
    return pl.pallas_call(
        functools.partial(_inproj_kernel, tm=tm, rc=rc),
        grid=(n // tm, N_IN // tn),
        in_specs=[
            pl.BlockSpec((tm, D_MODEL), lambda i, j: (i, 0)),
            pl.BlockSpec((1, D_MODEL), lambda i, j: (0, 0)),
            pl.BlockSpec((1, D_MODEL), lambda i, j: (0, 0)),
            pl.BlockSpec((D_MODEL, tn), lambda i, j: (0, j)),
        ],
        out_specs=pl.BlockSpec((tm, tn), lambda i, j: (i, j)),
        out_shape=jax.ShapeDtypeStruct((n, N_IN), BF16),
        scratch_shapes=[pltpu.VMEM((tm, D_MODEL), BF16)],
        compiler_params=_cparams(("parallel", "arbitrary")),
        name="ln_inproj",
    )(x, g, b, w_bf16)


def _rope(x, c, s):
    lane = lax.broadcasted_iota(I32, (1, HEAD_W), 1)
    first_half = (lane % HEAD_DIM) < (ROT_DIM // 2)
    xs = jnp.where(first_half, pltpu.roll(x, HEAD_W - ROT_DIM // 2, 1), pltpu.roll(x, ROT_DIM // 2, 1))
    return x * c + xs * s


ONES_ROWS = 16


def _attn_kernel(lam_ref, q_ref, k_ref, v_ref, cq_ref, sq_ref, ck_ref, sk_ref, g_ref, o_ref,
                 k_scr, vt_scr, q_scr, *, seq, tq, tk, out_scale):
    nkv = seq // tk

    @pl.when(pl.program_id(2) == 0)
    def _():
        def body(c, carry):
            r = pl.ds(pl.multiple_of(c * tk, tk), tk)
            k_scr[r, :] = _rope(k_ref[r, :].astype(F32), ck_ref[r, :], sk_ref[r, :]).astype(BF16)
            vt_scr[c, :HEAD_W, :] = v_ref[r, :].astype(F32).T.astype(BF16)
            vt_scr[c, HEAD_W:, :] = jnp.ones((ONES_ROWS, tk), BF16)
            return carry
        lax.fori_loop(0, nkv, body, 0)

    q = _rope(q_ref[...].astype(F32), cq_ref[...], sq_ref[...]) * (HEAD_DIM ** -0.5 * math.log2(math.e))
    lane = lax.broadcasted_iota(I32, (1, HEAD_W), 1)
    comp1 = lane < HEAD_DIM
    q_scr[pl.ds(0, tq), :] = jnp.where(comp1, q, 0.0).astype(BF16)
    q_scr[pl.ds(tq, tq), :] = jnp.where(comp1, 0.0, q).astype(BF16)

    def scores(j):
        return lax.dot_general(k_scr[j * tk:(j + 1) * tk, :], q_scr[...], (((1,), (1,)), ((), ())),
                               preferred_element_type=F32)

    m = jnp.full((1, 2 * tq), -jnp.inf, F32)
    acc = jnp.zeros((HEAD_W + ONES_ROWS, 2 * tq), F32)
    s_cur = scores(0)
    for j in range(nkv):
        s_next = scores(j + 1) if j + 1 < nkv else None
        m_new = jnp.maximum(m, jnp.max(s_cur, axis=0, keepdims=True))
        p = jnp.exp2(s_cur - m_new).astype(BF16)
        acc = jnp.exp2(m - m_new) * acc + jnp.dot(vt_scr[j], p, preferred_element_type=F32)
        m = m_new
        s_cur = s_next

    o = acc[:HEAD_W] / acc[HEAD_W:HEAD_W + 1]
    o = o[:, :tq] - lam_ref[0, 0] * o[:, tq:]
    o = o * lax.rsqrt(jnp.mean(o * o, axis=0, keepdims=True) + LN_EPS) * g_ref[...] * out_scale
    o_ref[...] = o.T.astype(BF16)


def _attention(proj, lam, cos_t, sin_t, subln_g, batch, seq, lambda_init):
    tq = min(256, seq)
    tk = min(1024, seq)
    nq = seq // tq
    kern = functools.partial(_attn_kernel, seq=seq, tq=tq, tk=tk, out_scale=1.0 - lambda_init)
    return pl.pallas_call(
        kern,
        grid=(batch, N_HEADS, nq),
        in_specs=[
            pl.BlockSpec(memory_space=pltpu.SMEM),
            pl.BlockSpec((tq, HEAD_W), lambda b, h, i: (b * nq + i, h)),
            pl.BlockSpec((seq, HEAD_W), lambda b, h, i: (b, N_HEADS + h)),
            pl.BlockSpec((seq, HEAD_W), lambda b, h, i: (b, 2 * N_HEADS + h)),
            pl.BlockSpec((tq, HEAD_W), lambda b, h, i: (i, 0)),
            pl.BlockSpec((tq, HEAD_W), lambda b, h, i: (i, 0)),
            pl.BlockSpec((seq, HEAD_W), lambda b, h, i: (0, 0)),
            pl.BlockSpec((seq, HEAD_W), lambda b, h, i: (0, 0)),
            pl.BlockSpec((HEAD_W, 1), lambda b, h, i: (0, 0)),
        ],
        out_specs=pl.BlockSpec((tq, HEAD_W), lambda b, h, i: (b * nq + i, h)),
        out_shape=jax.ShapeDtypeStruct((batch * seq, ATTN_W), BF16),
        scratch_shapes=[
            pltpu.VMEM((seq, HEAD_W), BF16),
            pltpu.VMEM((seq // tk, HEAD_W + ONES_ROWS, tk), BF16),
            pltpu.VMEM((2 * tq, HEAD_W), BF16),
        ],
        compiler_params=_cparams(("parallel", "parallel", "arbitrary")),
        name="diff_attention",
    )(lam, proj, proj, proj, cos_t, sin_t, cos_t, sin_t, subln_g.reshape(HEAD_W, 1))


def _rope_tables(seq):
    half = ROT_DIM // 2
    inv = ROPE_THETA ** (-jnp.arange(0, ROT_DIM, 2, dtype=F32) / ROT_DIM)
    ang = jnp.arange(seq, dtype=F32)[:, None] * inv[None, :]
    lane = jnp.arange(HEAD_W) % HEAD_DIM
    cos_l = jnp.cos(ang)[:, lane % half]
    sin_l = jnp.sin(ang)[:, lane % half]
    cos_t = jnp.where(lane[None, :] < ROT_DIM, cos_l, 1.0)
    sin_t = jnp.where(lane[None, :] < half, -sin_l, jnp.where(lane[None, :] < ROT_DIM, sin_l, 0.0))
    return cos_t.astype(F32), sin_t.astype(F32)


def _s5_operators(lam_re, lam_im, log_dt, b_re, b_im, c_re, c_im):
    L, G, P, CG = S5_CHUNK, SSM_GROUPS, SSM_STATE, SSM_GROUP
    hp = lax.Precision.HIGHEST
    lr = jnp.minimum(lam_re.astype(F32), -1e-4)
    li = lam_im.astype(F32)
    dt = jnp.exp(log_dt.astype(F32))[..., None]
    nn = jnp.arange(L + 1, dtype=F32)[:, None, None, None]
    mag = jnp.exp(nn * (lr * dt)[None])
    pw_r = mag * jnp.cos(nn * (li * dt)[None])
    pw_i = mag * jnp.sin(nn * (li * dt)[None])
    ab_r, ab_i = pw_r[1], pw_i[1]
    nr = ab_r - 1.0
    den = lr * lr + li * li
    cr_ = ((nr * lr + ab_i * li) / den)[..., None]
    ci_ = ((ab_i * lr - nr * li) / den)[..., None]
    br = b_re.astype(F32)
    bi = b_im.astype(F32)
    bb_r = cr_ * br - ci_ * bi
    bb_i = cr_ * bi + ci_ * br
    cr = c_re.astype(F32)
    ci = c_im.astype(F32)

    m_r = pw_r[..., None] * bb_r[None] - pw_i[..., None] * bb_i[None]
    m_i = pw_r[..., None] * bb_i[None] + pw_i[..., None] * bb_r[None]
    kk = (jnp.einsum('dgop,ndgpi->ndgoi', cr, m_r, precision=hp)
          - jnp.einsum('dgop,ndgpi->ndgoi', ci, m_i, precision=hp))
    ii = jnp.arange(L)[:, None]
    jj = jnp.arange(L)[None, :]
    kf = jnp.where((jj >= ii)[:, :, None, None, None], kk[:, 0][jnp.clip(jj - ii, 0, L)], 0.0)
    kb = jnp.where((ii >= jj)[:, :, None, None, None], kk[:, 1][jnp.clip(ii - jj, 0, L)], 0.0)
    t_op = jnp.transpose(kf + kb, (2, 0, 4, 1, 3)).reshape(G, S5_W, S5_W)

    def state_in(mr, mi):
        e_re = jnp.transpose(mr, (1, 0, 3, 2)).reshape(G, S5_W, P)
        e_im = jnp.transpose(mi, (1, 0, 3, 2)).reshape(G, S5_W, P)
        return jnp.concatenate([e_re, e_im, e_im, e_re], axis=-1)
    bm = jnp.concatenate([state_in(m_r[L - 1::-1, 0][:L], m_i[L - 1::-1, 0][:L]),
                          state_in(m_r[:L, 1], m_i[:L, 1])], axis=-1)

    def state_out(d, pows):
        pr = pw_r[pows, d]
        pi = pw_i[pows, d]
        g_r = cr[d][None] * pr[:, :, None, :] - ci[d][None] * pi[:, :, None, :]
        g_i = cr[d][None] * pi[:, :, None, :] + ci[d][None] * pr[:, :, None, :]
        rows_re = jnp.transpose(g_r, (1, 3, 0, 2)).reshape(G, P, S5_W)
        rows_im = jnp.transpose(-g_i, (1, 3, 0, 2)).reshape(G, P, S5_W)
        return jnp.concatenate([rows_re, rows_im], axis=1)
    cm = jnp.concatenate([state_out(0, jnp.arange(1, L + 1)), state_out(1, jnp.arange(L, 0, -1))], axis=1)

    def step_mult(d):
        ar = jnp.concatenate([pw_r[L, d]] * 4, axis=-1)
        ai = jnp.concatenate([-pw_i[L, d], pw_i[L, d], pw_i[L, d], -pw_i[L, d]], axis=-1)
        return ar, ai
    arf, aif = step_mult(0)
    arb, aib = step_mult(1)
    a_r = jnp.concatenate([arf, arb], axis=-1)[:, None, :]
    a_i = jnp.concatenate([aif, aib], axis=-1)[:, None, :]
    return t_op.astype(BF16), bm.astype(BF16), cm.astype(BF16), a_r, a_i


S5_LANE_GROUPS = LANES // SSM_GROUP
S5_CAT = S5_CHUNK * LANES


def _s5_kernel(u_ref, t_ref, bm_ref, cm_ref, ar_ref, ai_ref, d_ref, o_ref,
               perm_scr, stage_scr, uall_scr, e_scr, s_scr, *, nc, nb, seg):
    rows = nb * nc
    w2 = S5_W
    p2 = 2 * SSM_STATE

    @pl.when((pl.program_id(0) == 0) & (pl.program_id(1) == 0))
    def _():
        def body(t, carry):
            gc = lax.broadcasted_iota(I32, (LANES, S5_CAT), 0)
            tgt = (gc // SSM_GROUP) * S5_W + t * SSM_GROUP + gc % SSM_GROUP
            col = lax.broadcasted_iota(I32, (LANES, S5_CAT), 1)
            r0 = pl.multiple_of(t * LANES, LANES)
            perm_scr[pl.ds(r0, LANES), :] = jnp.where(col == tgt, 1.0, 0.0).astype(BF16)
            return carry
        lax.fori_loop(0, S5_CHUNK, body, 0)

    def relayout_in(k, carry):
        tok0 = pl.multiple_of(k * (seg * S5_CHUNK), seg * S5_CHUNK)
        stage_scr[...] = u_ref[pl.ds(tok0, seg * S5_CHUNK), :].astype(F32)
        z = jnp.concatenate([stage_scr[pl.ds(t, seg, stride=S5_CHUNK), :] for t in range(S5_CHUNK)], axis=1)
        r0 = pl.multiple_of(k * seg, seg)
        uall_scr[pl.ds(r0, seg), :] = jnp.dot(z.astype(BF16), perm_scr[...],
                                              preferred_element_type=F32).astype(BF16)
        return carry
    lax.fori_loop(0, rows // seg, relayout_in, 0)

    for gp in range(S5_LANE_GROUPS // 2):
        pair = (2 * gp, 2 * gp + 1)
        for q, g in enumerate(pair):
            e = jnp.dot(uall_scr[:, g * w2:(g + 1) * w2], bm_ref[g], preferred_element_type=F32)
            for j in range(4):
                e_scr[4 * q + j] = e[:, j * p2:(j + 1) * p2]
        mult = [[(ar_ref[g, :, j * p2:(j + 1) * p2], ai_ref[g, :, j * p2:(j + 1) * p2]) for j in range(4)]
                for g in pair]

        def body(c, carry):
            rf = pl.ds(c, nb, stride=nc)
            rb = pl.ds(nc - 1 - c, nb, stride=nc)
            out = []
            for q in range(2):
                xf, xfs, zb, zbs = carry[4 * q:4 * q + 4]
                m = mult[q]
                s_scr[2 * q, rf, :] = xf
                s_scr[2 * q + 1, rb, :] = zb
                out += [m[0][0] * xf + m[0][1] * xfs + e_scr[4 * q, rf, :],
                        m[1][0] * xfs + m[1][1] * xf + e_scr[4 * q + 1, rf, :],
                        m[2][0] * zb + m[2][1] * zbs + e_scr[4 * q + 2, rb, :],
                        m[3][0] * zbs + m[3][1] * zb + e_scr[4 * q + 3, rb, :]]
            return tuple(out)
        z0 = jnp.zeros((nb, p2), F32)
        lax.fori_loop(0, nc, body, (z0,) * 8)

        for q, g in enumerate(pair):
            ug = uall_scr[:, g * w2:(g + 1) * w2]
            st = jnp.concatenate([s_scr[2 * q], s_scr[2 * q + 1]], axis=1).astype(BF16)
            y = (jnp.dot(ug, t_ref[g], preferred_element_type=F32)
                 + jnp.dot(st, cm_ref[g], preferred_element_type=F32)
                 + d_ref[g] * ug.astype(F32))
            uall_scr[:, g * w2:(g + 1) * w2] = _gelu_tanh(y).astype(BF16)

    def relayout_out(k, carry):
        r0 = pl.multiple_of(k * seg, seg)
        yt = lax.dot_general(uall_scr[pl.ds(r0, seg), :], perm_scr[...], (((1,), (1,)), ((), ())),
                             preferred_element_type=F32)
        for t in range(S5_CHUNK):
            stage_scr[pl.ds(t, seg, stride=S5_CHUNK), :] = yt[:, t * LANES:(t + 1) * LANES]
        tok0 = pl.multiple_of(k * (seg * S5_CHUNK), seg * S5_CHUNK)
        o_ref[pl.ds(tok0, seg * S5_CHUNK), :] = stage_scr[...].astype(BF16)
        return carry
    lax.fori_loop(0, rows // seg, relayout_out, 0)


def _s5_branch(proj, ops, d_tiled, batch, seq):
    t_op, bm, cm, a_r, a_i = ops
    n = batch * seq
    nc = seq // S5_CHUNK
    rows_blk = min(16384, n)
    nb = rows_blk // seq
    seg = min(128, nb * nc)
    ucol = (3 * ATTN_W) // LANES
    lg = S5_LANE_GROUPS
    blk = lambda gb, sb: (gb, 0, 0)
    return pl.pallas_call(
        functools.partial(_s5_kernel, nc=nc, nb=nb, seg=seg),
        grid=(D_SSM // LANES, n // rows_blk),
        in_specs=[
            pl.BlockSpec((rows_blk, LANES), lambda gb, sb: (sb, ucol + gb)),
            pl.BlockSpec((lg, S5_W, S5_W), blk),
            pl.BlockSpec((lg, S5_W, 2 * S5_W), blk),
            pl.BlockSpec((lg, S5_W, S5_W), blk),
            pl.BlockSpec((lg, 1, 2 * S5_W), blk),
            pl.BlockSpec((lg, 1, 2 * S5_W), blk),
            pl.BlockSpec((lg, 1, S5_W), blk),
        ],
        out_specs=pl.BlockSpec((rows_blk, LANES), lambda gb, sb: (sb, gb)),
        out_shape=jax.ShapeDtypeStruct((n, D_SSM), BF16),
        scratch_shapes=[
            pltpu.VMEM((S5_CAT, S5_CAT), BF16),
            pltpu.VMEM((seg * S5_CHUNK, LANES), F32),
            pltpu.VMEM((nb * nc, S5_CAT), BF16),
            pltpu.VMEM((8, nb * nc, LANES), F32),
            pltpu.VMEM((4, nb * nc, LANES), F32),
        ],
        compiler_params=_cparams(("arbitrary", "arbitrary")),
        name="s5_chunked",
    )(proj, t_op, bm, cm, a_r, a_i, d_tiled)


def _merge_kernel(y_ref, a_ref, gs_ref, ga_ref, wglu_ref, bglu_ref, wap_ref, wsp_ref, o_ref):
    y = y_ref[...]
    z = jnp.dot(y, wglu_ref[...], preferred_element_type=F32) + bglu_ref[...]
    act = (y.astype(F32) * _sigmoid(z)).astype(BF16)
    pa = jnp.dot(a_ref[...], wap_ref[...], preferred_element_type=F32)
    ps = jnp.dot(act, wsp_ref[...], preferred_element_type=F32)
    o = _sigmoid(ga_ref[...].astype(F32)) * pa + _sigmoid(gs_ref[...].astype(F32)) * ps
    o_ref[...] = o.astype(BF16)


def _merge(yact, attn, proj, wglu, bglu, wap, wsp):
    n = yact.shape[0]
    tm = min(256, n)
    gcol = (3 * ATTN_W + D_SSM) // D_MODEL
    const = lambda i: (0, 0)
    return pl.pallas_call(
        _merge_kernel,
        grid=(n // tm,),
        in_specs=[
            pl.BlockSpec((tm, D_SSM), lambda i: (i, 0)),
            pl.BlockSpec((tm, ATTN_W), lambda i: (i, 0)),
            pl.BlockSpec((tm, D_MODEL), lambda i: (i, gcol)),
            pl.BlockSpec((tm, D_MODEL), lambda i: (i, gcol + 1)),
            pl.BlockSpec((D_SSM, D_SSM), const),
            pl.BlockSpec((1, D_SSM), const),
            pl.BlockSpec((ATTN_W, D_MODEL), const),
            pl.BlockSpec((D_SSM, D_MODEL), const),
        ],
        out_specs=pl.BlockSpec((tm, D_MODEL), lambda i: (i, 0)),
        out_shape=jax.ShapeDtypeStruct((n, D_MODEL), BF16),
        compiler_params=_cparams(("parallel",)),
        name="glu_merge",
    )(yact, attn, proj, proj, wglu, bglu, wap, wsp)


def _outproj_kernel(x_ref, m_ref, wout_ref, gin_ref, bin_ref, g1_ref, b1_ref, wr_ref, br_ref,
                    x1_ref, xp_ref, te_ref, tg_ref):
    xn = _ln(x_ref[...], gin_ref[...], bin_ref[...])
    z = DN_ALPHA * xn + jnp.dot(m_ref[...], wout_ref[...], preferred_element_type=F32)
    x1 = _ln(z, g1_ref[...], b1_ref[...])
    x1_ref[...] = x1
    _store_row_tiles(xp_ref, 0, x1.shape[0], _pack_rows(x1))

    x_hi = x1.astype(BF16)
    x_lo = (x1 - x_hi.astype(F32)).astype(BF16)
    r = (jnp.dot(x_hi, wr_ref[...], preferred_element_type=F32)
         + jnp.dot(x_lo, wr_ref[...], preferred_element_type=F32))
    logits = r[:, :N_EXPERTS] + r[:, N_EXPERTS:] + br_ref[...]
    e_iota = lax.broadcasted_iota(I32, logits.shape, 1)
    k_iota = lax.broadcasted_iota(I32, (logits.shape[0], TOP_K), 1)
    vals = jnp.zeros((logits.shape[0], TOP_K), F32)
    idxs = jnp.zeros((logits.shape[0], TOP_K), I32)
    cur = logits
    for k in range(TOP_K):
        m = jnp.max(cur, axis=-1, keepdims=True)
        idx = jnp.min(jnp.where(cur == m, e_iota, N_EXPERTS), axis=-1, keepdims=True)
        vals = jnp.where(k_iota == k, m, vals)
        idxs = jnp.where(k_iota == k, idx, idxs)
        cur = jnp.where(e_iota == idx, -jnp.inf, cur)
    ex = jnp.exp(vals - jnp.max(vals, axis=-1, keepdims=True))
    te_ref[...] = idxs
    tg_ref[...] = ex / jnp.sum(ex, axis=-1, keepdims=True)


def _outproj(x, merged, wout, gin, bin_, g1, b1, wr, br):
    n = x.shape[0]
    tm = min(256, n)
    const = lambda i: (0, 0)
    row = lambda i: (i, 0)
    return pl.pallas_call(
        _outproj_kernel,
        grid=(n // tm,),
        in_specs=[
            pl.BlockSpec((tm, D_MODEL), row),
            pl.BlockSpec((tm, D_MODEL), row),
            pl.BlockSpec((D_MODEL, D_MODEL), const),
            pl.BlockSpec((1, D_MODEL), const),
            pl.BlockSpec((1, D_MODEL), const),
            pl.BlockSpec((1, D_MODEL), const),
            pl.BlockSpec((1, D_MODEL), const),
            pl.BlockSpec((D_MODEL, 2 * N_EXPERTS), const),
            pl.BlockSpec((1, N_EXPERTS), const),
        ],
        out_specs=[
            pl.BlockSpec((tm, D_MODEL), row),
            pl.BlockSpec((tm * ROW_TILE, LANES), row),
            pl.BlockSpec((tm, TOP_K), row),
            pl.BlockSpec((tm, TOP_K), row),
        ],
        out_shape=[
            jax.ShapeDtypeStruct((n, D_MODEL), F32),
            jax.ShapeDtypeStruct((n * ROW_TILE, LANES), U32),
            jax.ShapeDtypeStruct((n, TOP_K), I32),
            jax.ShapeDtypeStruct((n, TOP_K), F32),
        ],
        compiler_params=_cparams(("parallel",)),
        name="outproj_ln_router",
    )(x, merged, wout, gin, bin_, g1, b1, wr, br)


def _gather_rows(idx_smem, base, src_hbm, dst_vmem, sem, count):
    def body(i, carry):
        src_row = pl.multiple_of(idx_smem[base + i], ROW_TILE)
        dst_row = pl.multiple_of(i * ROW_TILE, ROW_TILE)
        pltpu.make_async_copy(src_hbm.at[pl.ds(src_row, ROW_TILE)], dst_vmem.at[pl.ds(dst_row, ROW_TILE)],
                              sem).start()
        return carry
    lax.fori_loop(0, count, body, 0, unroll=8)


def _wait_rows(src_hbm, dst_vmem, sem, count):
    pltpu.make_async_copy(src_hbm.at[pl.ds(0, count * ROW_TILE)], dst_vmem, sem).wait()


def _prefetch_gather(step, nsteps, idx_hbm, idx_smem, idx_sem, src_hbm, gbuf, gsem, count):
    slot = step % 2
    nxt = 1 - slot

    def idx_copy(s, sl):
        return pltpu.make_async_copy(idx_hbm.at[pl.ds(pl.multiple_of(s * count, count), count)],
                                     idx_smem.at[pl.ds(pl.multiple_of(sl * count, count), count)],
                                     idx_sem.at[sl])

    @pl.when(step == 0)
    def _():
        c = idx_copy(0, 0)
        c.start()
        c.wait()
        _gather_rows(idx_smem, 0, src_hbm, gbuf.at[0], gsem.at[0], count)

        @pl.when(nsteps > 1)
        def _():
            idx_copy(1, 1).start()

    _wait_rows(src_hbm, gbuf.at[slot], gsem.at[slot], count)

    @pl.when(step + 1 < nsteps)
    def _():
        idx_copy(step + 1, nxt).wait()
        _gather_rows(idx_smem, nxt * count, src_hbm, gbuf.at[nxt], gsem.at[nxt], count)

    @pl.when(step + 2 < nsteps)
    def _():
        idx_copy(step + 2, slot).start()


def _moe_kernel(be_ref, nu_ref, sh_ref, idx_hbm, x_hbm, wg_ref, bg_ref, wu_ref, bu_ref, wd_ref, bd_ref, o_ref,
                idx_smem, idx_sem, gbuf, gsem, xb_scr, h_scr, *, tmb, rc, nf, nn, tf, tn):
    b = pl.program_id(0)
    t = pl.program_id(1)
    n_used = nu_ref[0]
    active = b < n_used
    slot = b % 2
    nxt = 1 - slot
    per = tmb // (nf + nn)
    win = 2 * tmb

    def idx_copy(s, sl):
        return pltpu.make_async_copy(idx_hbm.at[pl.ds(pl.multiple_of(s * win, win), win)],
                                     idx_smem.at[pl.ds(pl.multiple_of(sl * win, win), win)],
                                     idx_sem.at[sl])

    def gather_next_slice():
        base = nxt * win + sh_ref[b + 1] + t * per
        for i in range(per):
            src_row = pl.multiple_of(idx_smem[base + i], ROW_TILE)
            dst_row = pl.multiple_of((t * per + i) * ROW_TILE, ROW_TILE)
            pltpu.make_async_copy(x_hbm.at[pl.ds(src_row, ROW_TILE)], gbuf.at[nxt, pl.ds(dst_row, ROW_TILE)],
                                  gsem.at[nxt]).start()

    @pl.when((b <= n_used) & (t == 0))
    def _():
        @pl.when(b == 0)
        def _():
            c = idx_copy(0, 0)
            c.start()
            c.wait()
            _gather_rows(idx_smem, sh_ref[0], x_hbm, gbuf.at[0], gsem.at[0], tmb)
            idx_copy(1, 1).start()

        _wait_rows(x_hbm, gbuf.at[slot], gsem.at[slot], tmb)

    @pl.when(active & (t == 0))
    def _():
        idx_copy(b + 1, nxt).wait()

        @pl.when(b + 2 <= n_used)
        def _():
            idx_copy(b + 2, slot).start()

        def body(c, carry):
            r0 = pl.multiple_of(c * rc, rc)
            r = pl.ds(r0, rc)
            for s in range(ROW_TILE):
                c_lo, c_hi = _packed_chunks(s, D_MODEL)
                lo, hi = _unpack_rows(_load_row_tile_words(gbuf.at[slot], r0, rc, s))
                xb_scr[r, c_lo * LANES:(c_lo + 1) * LANES] = lo.astype(BF16)
                xb_scr[r, c_hi * LANES:(c_hi + 1) * LANES] = hi.astype(BF16)
            return carry
        lax.fori_loop(0, tmb // rc, body, 0)

    @pl.when(active & (t < nf))
    def _():
        gather_next_slice()
        x = xb_scr[...]
        g = jnp.dot(x, wg_ref[0], preferred_element_type=F32) + bg_ref[0]
        u = jnp.dot(x, wu_ref[0], preferred_element_type=F32) + bu_ref[0]
        g = jnp.minimum(g, SWIGLU_LIMIT)
        u = jnp.clip(u, -SWIGLU_LIMIT, SWIGLU_LIMIT)
        h = (g * _sigmoid(SWIGLU_ALPHA * g) * (u + 1.0)).astype(BF16)
        for f in range(nf):
            @pl.when(t == f)
            def _():
                h_scr[:, f * tf:(f + 1) * tf] = h

    @pl.when(active & (t >= nf))
    def _():
        gather_next_slice()
        y = jnp.dot(h_scr[...], wd_ref[0], preferred_element_type=F32) + bd_ref[0]
        sub_per_tile = tn // (2 * LANES)
        s0 = (t - nf) * sub_per_tile
        for c in range(tmb // rc):
            yc = y[c * rc:(c + 1) * rc]
            for q in range(sub_per_tile):
                lo = yc[:, q * LANES:(q + 1) * LANES]
                hi = yc[:, tn // 2 + q * LANES:tn // 2 + (q + 1) * LANES]
                o_ref[pl.ds(c * rc * ROW_TILE + s0 + q, rc, stride=ROW_TILE), :] = _pack_words(lo, hi)

    @pl.when(jnp.logical_not(active) & (t == nf + nn - 1))
    def _():
        o_ref[...] = jnp.zeros(o_ref.shape, U32)


def _moe_experts(block_e, n_used, shift, row_win, x_packed, wg, bg, wu, bu, wd, bd, tmb):
    nblk = block_e.shape[0]
    tf = MOE_TILE
    tn = MOE_TILE
    nf = D_FF // tf
    nn = D_MODEL // tn
    rc = min(256, tmb)

    def fsel(b, t, nu):
        return jnp.where(b < nu[0], jnp.minimum(t, nf - 1), nf - 1)

    def nsel(b, t, nu):
        return jnp.where(b < nu[0], jnp.maximum(t - nf, 0), nn - 1)

    grid_spec = pltpu.PrefetchScalarGridSpec(
        num_scalar_prefetch=3,
        grid=(nblk, nf + nn),
        in_specs=[
            pl.BlockSpec(memory_space=pl.ANY),
            pl.BlockSpec(memory_space=pl.ANY),
            pl.BlockSpec((1, D_MODEL, tf), lambda b, t, be, nu, sh: (be[b], 0, fsel(b, t, nu))),
            pl.BlockSpec((1, 1, tf), lambda b, t, be, nu, sh: (be[b], 0, fsel(b, t, nu))),
            pl.BlockSpec((1, D_MODEL, tf), lambda b, t, be, nu, sh: (be[b], 0, fsel(b, t, nu))),
            pl.BlockSpec((1, 1, tf), lambda b, t, be, nu, sh: (be[b], 0, fsel(b, t, nu))),
            pl.BlockSpec((1, D_FF, tn), lambda b, t, be, nu, sh: (be[b], 0, nsel(b, t, nu))),
            pl.BlockSpec((1, 1, tn), lambda b, t, be, nu, sh: (be[b], 0, nsel(b, t, nu))),
        ],
        out_specs=pl.BlockSpec((tmb * ROW_TILE, LANES), lambda b, t, be, nu, sh: (b, 0)),
        scratch_shapes=[
            pltpu.SMEM((4 * tmb,), I32),
            pltpu.SemaphoreType.DMA((2,)),
            pltpu.VMEM((2, tmb * ROW_TILE, LANES), U32),
            pltpu.SemaphoreType.DMA((2,)),
            pltpu.VMEM((tmb, D_MODEL), BF16),
            pltpu.VMEM((tmb, D_FF), BF16),
        ],
    )
    return pl.pallas_call(
        functools.partial(_moe_kernel, tmb=tmb, rc=rc, nf=nf, nn=nn, tf=tf, tn=tn),
        grid_spec=grid_spec,
        out_shape=jax.ShapeDtypeStruct((nblk * tmb * ROW_TILE, LANES), U32),
        compiler_params=_cparams(("arbitrary", "arbitrary"), disable_bounds_checks=True),
        name="moe_experts",
    )(block_e, n_used, shift, row_win, x_packed, wg, bg, wu, bu, wd, bd)


def _combine_kernel(idx_hbm, y_hbm, x1_ref, gate_ref, g_ref, b_ref, o_ref,
                    idx_smem, idx_sem, gbuf, gsem, *, tc):
    i = pl.program_id(0)
    _prefetch_gather(i, pl.num_programs(0), idx_hbm, idx_smem, idx_sem, y_hbm, gbuf, gsem, TOP_K * tc)
    slot = i % 2
    gates = gate_ref[...]
    gk = [gates[:, k:k + 1] for k in range(TOP_K)]
    ncol = D_MODEL // LANES
    z = [DN_ALPHA * x1_ref[:, c * LANES:(c + 1) * LANES] for c in range(ncol)]
    for s in range(ROW_TILE):
        c_lo, c_hi = _packed_chunks(s, MOE_TILE)
        for k in range(TOP_K):
            lo, hi = _unpack_rows(_load_row_tile_words(gbuf.at[slot], k * tc, tc, s))
            z[c_lo] = z[c_lo] + gk[k] * lo
            z[c_hi] = z[c_hi] + gk[k] * hi
    mu = sum(jnp.sum(zc, axis=-1, keepdims=True) for zc in z) / D_MODEL
    z = [zc - mu for zc in z]
    var = sum(jnp.sum(zc * zc, axis=-1, keepdims=True) for zc in z) / D_MODEL
    inv = lax.rsqrt(var + LN_EPS)
    for c in range(ncol):
        cols = slice(c * LANES, (c + 1) * LANES)
        o_ref[:, cols] = z[c] * inv * g_ref[:, cols] + b_ref[:, cols]


def _combine(dest_tiles, yb, x1, gates, g2, b2, tc):
    n = x1.shape[0]
    const = lambda i: (0, 0)
    row = lambda i: (i, 0)
    return pl.pallas_call(
        functools.partial(_combine_kernel, tc=tc),
        grid=(n // tc,),
        in_specs=[
            pl.BlockSpec(memory_space=pl.ANY),
            pl.BlockSpec(memory_space=pl.ANY),
            pl.BlockSpec((tc, D_MODEL), row),
            pl.BlockSpec((tc, TOP_K), row),
            pl.BlockSpec((1, D_MODEL), const),
            pl.BlockSpec((1, D_MODEL), const),
        ],
        out_specs=pl.BlockSpec((tc, D_MODEL), row),
        out_shape=jax.ShapeDtypeStruct((n, D_MODEL), F32),
        scratch_shapes=[
            pltpu.SMEM((2 * TOP_K * tc,), I32),
            pltpu.SemaphoreType.DMA((2,)),
            pltpu.VMEM((2, TOP_K * tc * ROW_TILE, LANES), U32),
            pltpu.SemaphoreType.DMA((2,)),
        ],
        compiler_params=_cparams(("arbitrary",), disable_bounds_checks=True),
        name="moe_combine_ln",
    )(dest_tiles, yb, x1, gates, g2, b2)


def _route_meta(top_e, tmb):
    nt = top_e.shape[0]
    nblk = (nt * TOP_K) // tmb + N_EXPERTS
    e_ids = jnp.arange(N_EXPERTS, dtype=I32)
    sel = top_e[:, :, None] == e_ids[None, None, :]
    onehot = sel.sum(axis=1).astype(I32)
    incl = jnp.cumsum(onehot, axis=0)
    counts = incl[-1]
    padded = (counts + tmb - 1) // tmb * tmb
    pend = jnp.cumsum(padded)
    pstart = pend - padded
    start = jnp.cumsum(counts) - counts
    dest = jnp.where(sel, (pstart[None, :] + incl - onehot)[:, None, :], 0).sum(axis=-1).astype(I32)
    n_used = (pend[-1] // tmb).astype(I32)
    blk = jnp.arange(nblk, dtype=I32)
    be = jnp.clip(jnp.searchsorted(pend, blk * tmb, side='right'), 0, N_EXPERTS - 1).astype(I32)
    be = jnp.where(blk < n_used, be, be[jnp.maximum(n_used - 1, 0)])
    order = jnp.argsort(top_e.reshape(-1), stable=True).astype(I32)
    lines = (nt * TOP_K) // LANES + 2 * tmb // LANES + 1
    tok_sorted = jnp.zeros((lines * LANES,), I32).at[:nt * TOP_K].set((order // TOP_K) * ROW_TILE)
    win = jnp.clip(blk * tmb - pstart[be] + start[be], 0, nt * TOP_K)
    line_idx = (win // LANES)[:, None] + jnp.arange(2 * tmb // LANES, dtype=I32)[None, :]
    row_win = jnp.take(tok_sorted.reshape(lines, LANES), line_idx, axis=0).reshape(-1)
    shift = (win % LANES).astype(I32)
    return be, n_used.reshape(1), shift, row_win, dest * ROW_TILE


def kernel(x_prompt, x_sample, ln_in_g, ln_in_b, w_in, lam_q1, lam_k1, lam_q2, lam_k2, subln_g, w_attn_proj, ssm_lam_re, ssm_lam_im, ssm_log_dt, ssm_b_re, ssm_b_im, ssm_c_re, ssm_c_im, ssm_d, w_glu, b_glu, w_ssm_proj, w_out, ln1_g, ln1_b, w_router, b_router, w_gate, b_gate, w_up, b_up, w_down, b_down, ln2_g, ln2_b):
    l = 0
    row = lambda v: v.reshape(1, -1).astype(F32)
    lambda_init = 0.8 - 0.6 * math.exp(-0.3 * l)
    lam = (jnp.exp(jnp.sum(lam_q1[l].astype(F32) * lam_k1[l].astype(F32)))
           - jnp.exp(jnp.sum(lam_q2[l].astype(F32) * lam_k2[l].astype(F32))) + lambda_init).reshape(1, 1)
    w_in_b = w_in[l].astype(BF16)
    wglu_b = w_glu[l].astype(BF16)
    wap_b = w_attn_proj[l].astype(BF16)
    wsp_b = w_ssm_proj[l].astype(BF16)
    wout_b = w_out[l].astype(BF16)
    wg_b = w_gate[l].astype(BF16)
    wu_b = w_up[l].astype(BF16)
    wd_b = w_down[l].astype(BF16)
    wr = w_router[l].astype(F32)
    wr_hi = wr.astype(BF16)
    wr_split = jnp.concatenate([wr_hi, (wr - wr_hi.astype(F32)).astype(BF16)], axis=1)
    s5_ops = _s5_operators(ssm_lam_re[l], ssm_lam_im[l], ssm_log_dt[l], ssm_b_re[l], ssm_b_im[l],
                           ssm_c_re[l], ssm_c_im[l])
    d_tiled = jnp.tile(ssm_d[l].astype(F32).reshape(SSM_GROUPS, 1, SSM_GROUP), (1, 1, S5_CHUNK))

    def pre_moe(x3):
        batch, seq, _ = x3.shape
        x = x3.reshape(batch * seq, D_MODEL)
        proj = _inproj(x, row(ln_in_g), row(ln_in_b), w_in_b)
        cos_t, sin_t = _rope_tables(seq)
        attn = _attention(proj, lam, cos_t, sin_t, row(subln_g[l]), batch, seq, lambda_init)
        yact = _s5_branch(proj, s5_ops, d_tiled, batch, seq)
        merged = _merge(yact, attn, proj, wglu_b, row(b_glu[l]), wap_b, wsp_b)
        return _outproj(x, merged, wout_b, row(ln_in_g), row(ln_in_b), row(ln1_g[l]), row(ln1_b[l]),
                        wr_split, row(b_router[l]))

    parts = [pre_moe(x_prompt), pre_moe(x_sample)]
    xp_all = jnp.concatenate([p[1] for p in parts], axis=0)
    te_all = jnp.concatenate([p[2] for p in parts], axis=0)
    nt = xp_all.shape[0]
    tmb = 1024 if nt >= 8192 else 128
    be, n_used, shift, row_win, dest = _route_meta(te_all, tmb)
    yb = _moe_experts(be, n_used, shift, row_win, xp_all, wg_b, b_gate[l].astype(F32)[:, None, :],
                      wu_b, b_up[l].astype(F32)[:, None, :], wd_b, b_down[l].astype(F32)[:, None, :], tmb)

    outs = []
    off = 0
    for x3, p in zip((x_prompt, x_sample), parts):
        n = p[0].shape[0]
        tc = min(256, n)
        d = dest[off:off + n].reshape(n // tc, tc, TOP_K)
        d = jnp.transpose(d, (0, 2, 1)).reshape(-1)
        out = _combine(d, yb, p[0], p[3], row(ln2_g[l]), row(ln2_b[l]), tc)
        outs.append(out.reshape(x3.shape))
        off += n
    return tuple(outs)
```

```python
import functools
import math

import jax
import jax.numpy as jnp
from jax import lax
from jax.experimental import pallas as pl
from jax.experimental.pallas import tpu as pltpu

F32 = jnp.float32
BF16 = jnp.bfloat16
U32 = jnp.uint32
I32 = jnp.int32

D_MODEL = 2048
DEPTH = 1
N_HEADS = 8
HEAD_DIM = 64
HEAD_W = 2 * HEAD_DIM
ATTN_W = N_HEADS * HEAD_W
ROT_DIM = HEAD_DIM // 4
ROPE_THETA = 500000.0
SSM_GROUP = 16
SSM_GROUPS = 64
D_SSM = SSM_GROUP * SSM_GROUPS
SSM_STATE = 64
S5_CHUNK = 16
S5_W = S5_CHUNK * SSM_GROUP
N_IN = 3 * ATTN_W + D_SSM + 2 * D_MODEL
N_EXPERTS = 32
TOP_K = 4
D_FF = D_MODEL
SWIGLU_LIMIT = 7.0
SWIGLU_ALPHA = 1.702
DN_ALPHA = (2.0 * DEPTH) ** 0.25
LN_EPS = 1e-5
HALF = D_MODEL // 2
LANES = 128
ROW_TILE = 8
MOE_TILE = 512

VMEM_LIMIT = 56 * 1024 * 1024


def _cparams(sem, **kw):
    return pltpu.CompilerParams(dimension_semantics=sem, vmem_limit_bytes=VMEM_LIMIT, **kw)


def _ln(x, g, b):
    mu = jnp.mean(x, axis=-1, keepdims=True)
    xc = x - mu
    var = jnp.mean(xc * xc, axis=-1, keepdims=True)
    return xc * lax.rsqrt(var + LN_EPS) * g + b


def _sigmoid(x):
    return 1.0 / (1.0 + jnp.exp(-x))


def _gelu_tanh(x):
    return 0.5 * x * (1.0 + jnp.tanh(math.sqrt(2.0 / math.pi) * (x + 0.044715 * (x * x * x))))


def _pack_words(lo, hi):
    lo = lax.bitcast_convert_type(lo.astype(BF16).astype(F32), U32)
    hi = lax.bitcast_convert_type(hi.astype(BF16).astype(F32), U32)
    return (hi & jnp.uint32(0xFFFF0000)) | (lo >> 16)


def _pack_rows(x):
    return _pack_words(x[:, :HALF], x[:, HALF:])


def _packed_chunks(s, tile):
    sub_per_tile = tile // (2 * LANES)
    n, q = divmod(s, sub_per_tile)
    lo = n * (tile // LANES) + q
    return lo, lo + sub_per_tile


def _unpack_rows(w):
    lo = lax.bitcast_convert_type(w << 16, F32)
    hi = lax.bitcast_convert_type(w & jnp.uint32(0xFFFF0000), F32)
    return lo, hi


def _store_row_tiles(ref, r0, nrows, packed):
    for s in range(ROW_TILE):
        ref[pl.ds(r0 * ROW_TILE + s, nrows, stride=ROW_TILE), :] = packed[:, s * LANES:(s + 1) * LANES]


def _load_row_tile_words(ref, r0, nrows, s):
    return ref[pl.ds(r0 * ROW_TILE + s, nrows, stride=ROW_TILE), :]


def _inproj_kernel(x_ref, g_ref, b_ref, w_ref, o_ref, xn_ref, *, tm, rc):
    @pl.when(pl.program_id(1) == 0)
    def _():
        def body(c, carry):
            r = pl.ds(pl.multiple_of(c * rc, rc), rc)
            xn_ref[r, :] = _ln(x_ref[r, :], g_ref[...], b_ref[...]).astype(BF16)
            return carry
        lax.fori_loop(0, tm // rc, body, 0)

    o_ref[...] = jnp.dot(xn_ref[...], w_ref[...], preferred_element_type=F32).astype(BF16)


def _inproj(x, g, b, w_bf16):
    n = x.shape[0]
    tm = min(1024, n)
    tn = 1024
    rc = min(256, tm)
    return pl.pallas_call(
        functools.partial(_inproj_kernel, tm=tm, rc=rc),
        grid=(n // tm, N_IN // tn),
        in_specs=[
            pl.BlockSpec((tm, D_MODEL), lambda i, j: (i, 0)),
            pl.BlockSpec((1, D_MODEL), lambda i, j: (0, 0)),
            pl.BlockSpec((1, D_MODEL), lambda i, j: (0, 0)),
            pl.BlockSpec((D_MODEL, tn), lambda i, j: (0, j)),
        ],
        out_specs=pl.BlockSpec((tm, tn), lambda i, j: (i, j)),
        out_shape=jax.ShapeDtypeStruct((n, N_IN), BF16),
        scratch_shapes=[pltpu.VMEM((tm, D_MODEL), BF16)],
        compiler_params=_cparams(("parallel", "arbitrary")),
        name="ln_inproj",
    )(x, g, b, w_bf16)


def _rope(x, c, s):
    lane = lax.broadcasted_iota(I32, (1, HEAD_W), 1)
    first_half = (lane % HEAD_DIM) < (ROT_DIM // 2)
    xs = jnp.where(first_half, pltpu.roll(x, HEAD_W - ROT_DIM // 2, 1), pltpu.roll(x, ROT_DIM // 2, 1))
    return x * c + xs * s


ONES_ROWS = 16


def _attn_kernel(lam_ref, q_ref, k_ref, v_ref, cq_ref, sq_ref, ck_ref, sk_ref, g_ref, o_ref,
                 k_scr, vt_scr, q_scr, *, seq, tq, tk, out_scale):
    nkv = seq // tk

    @pl.when(pl.program_id(2) == 0)
    def _():
        def body(c, carry):
            r = pl.ds(pl.multiple_of(c * tk, tk), tk)
            k_scr[r, :] = _rope(k_ref[r, :].astype(F32), ck_ref[r, :], sk_ref[r, :]).astype(BF16)
            vt_scr[c, :HEAD_W, :] = v_ref[r, :].astype(F32).T.astype(BF16)
            vt_scr[c, HEAD_W:, :] = jnp.ones((ONES_ROWS, tk), BF16)
            return carry
        lax.fori_loop(0, nkv, body, 0)

    q = _rope(q_ref[...].astype(F32), cq_ref[...], sq_ref[...]) * (HEAD_DIM ** -0.5 * math.log2(math.e))
    lane = lax.broadcasted_iota(I32, (1, HEAD_W), 1)
    comp1 = lane < HEAD_DIM
    q_scr[pl.ds(0, tq), :] = jnp.where(comp1, q, 0.0).astype(BF16)
    q_scr[pl.ds(tq, tq), :] = jnp.where(comp1, 0.0, q).astype(BF16)

    def scores(j):
        return lax.dot_general(k_scr[j * tk:(j + 1) * tk, :], q_scr[...], (((1,), (1,)), ((), ())),
                               preferred_element_type=F32)

    m = jnp.full((1, 2 * tq), -jnp.inf, F32)
    acc = jnp.zeros((HEAD_W + ONES_ROWS, 2 * tq), F32)
    s_cur = scores(0)
    for j in range(nkv):
        s_next = scores(j + 1) if j + 1 < nkv else None
        m_new = jnp.maximum(m, jnp.max(s_cur, axis=0, keepdims=True))
        p = jnp.exp2(s_cur - m_new).astype(BF16)
        acc = jnp.exp2(m - m_new) * acc + jnp.dot(vt_scr[j], p, preferred_element_type=F32)
        m = m_new
        s_cur = s_next

    o = acc[:HEAD_W] / acc[HEAD_W:HEAD_W + 1]
    o = o[:, :tq] - lam_ref[0, 0] * o[:, tq:]
    o = o * lax.rsqrt(jnp.mean(o * o, axis=0, keepdims=True) + LN_EPS) * g_ref[...] * out_scale
    o_ref[...] = o.T.astype(BF16)


def _attention(proj, lam, cos_t, sin_t, subln_g, batch, seq, lambda_init):
    tq = min(512, seq)
    tk = min(1024, seq)
    nq = seq // tq
    kern = functools.partial(_attn_kernel, seq=seq, tq=tq, tk=tk, out_scale=1.0 - lambda_init)
    return pl.pallas_call(
        kern,
        grid=(batch, N_HEADS, nq),
        in_specs=[
            pl.BlockSpec(memory_space=pltpu.SMEM),
            pl.BlockSpec((tq, HEAD_W), lambda b, h, i: (b * nq + i, h)),
            pl.BlockSpec((seq, HEAD_W), lambda b, h, i: (b, N_HEADS + h)),
            pl.BlockSpec((seq, HEAD_W), lambda b, h, i: (b, 2 * N_HEADS + h)),
            pl.BlockSpec((tq, HEAD_W), lambda b, h, i: (i, 0)),
            pl.BlockSpec((tq, HEAD_W), lambda b, h, i: (i, 0)),
            pl.BlockSpec((seq, HEAD_W), lambda b, h, i: (0, 0)),
            pl.BlockSpec((seq, HEAD_W), lambda b, h, i: (0, 0)),
            pl.BlockSpec((HEAD_W, 1), lambda b, h, i: (0, 0)),
        ],
        out_specs=pl.BlockSpec((tq, HEAD_W), lambda b, h, i: (b * nq + i, h)),
        out_shape=jax.ShapeDtypeStruct((batch * seq, ATTN_W), BF16),
        scratch_shapes=[
            pltpu.VMEM((seq, HEAD_W), BF16),
            pltpu.VMEM((seq // tk, HEAD_W + ONES_ROWS, tk), BF16),
            pltpu.VMEM((2 * tq, HEAD_W), BF16),
        ],
        compiler_params=_cparams(("parallel", "parallel", "arbitrary")),
        name="diff_attention",
    )(lam, proj, proj, proj, cos_t, sin_t, cos_t, sin_t, subln_g.reshape(HEAD_W, 1))


def _rope_tables(seq):
    half = ROT_DIM // 2
    inv = ROPE_THETA ** (-jnp.arange(0, ROT_DIM, 2, dtype=F32) / ROT_DIM)
    ang = jnp.arange(seq, dtype=F32)[:, None] * inv[None, :]
    lane = jnp.arange(HEAD_W) % HEAD_DIM
    cos_l = jnp.cos(ang)[:, lane % half]
    sin_l = jnp.sin(ang)[:, lane % half]
    cos_t = jnp.where(lane[None, :] < ROT_DIM, cos_l, 1.0)
    sin_t = jnp.where(lane[None, :] < half, -sin_l, jnp.where(lane[None, :] < ROT_DIM, sin_l, 0.0))
    return cos_t.astype(F32), sin_t.astype(F32)


def _s5_operators(lam_re, lam_im, log_dt, b_re, b_im, c_re, c_im):
    L, G, P, CG = S5_CHUNK, SSM_GROUPS, SSM_STATE, SSM_GROUP
    hp = lax.Precision.HIGHEST
    lr = jnp.minimum(lam_re.astype(F32), -1e-4)
    li = lam_im.astype(F32)
    dt = jnp.exp(log_dt.astype(F32))[..., None]
    nn = jnp.arange(L + 1, dtype=F32)[:, None, None, None]
    mag = jnp.exp(nn * (lr * dt)[None])
    pw_r = mag * jnp.cos(nn * (li * dt)[None])
    pw_i = mag * jnp.sin(nn * (li * dt)[None])
    ab_r, ab_i = pw_r[1], pw_i[1]
    nr = ab_r - 1.0
    den = lr * lr + li * li
    cr_ = ((nr * lr + ab_i * li) / den)[..., None]
    ci_ = ((ab_i * lr - nr * li) / den)[..., None]
    br = b_re.astype(F32)
    bi = b_im.astype(F32)
    bb_r = cr_ * br - ci_ * bi
    bb_i = cr_ * bi + ci_ * br
    cr = c_re.astype(F32)
    ci = c_im.astype(F32)

    m_r = pw_r[..., None] * bb_r[None] - pw_i[..., None] * bb_i[None]
    m_i = pw_r[..., None] * bb_i[None] + pw_i[..., None] * bb_r[None]
    kk = (jnp.einsum('dgop,ndgpi->ndgoi', cr, m_r, precision=hp)
          - jnp.einsum('dgop,ndgpi->ndgoi', ci, m_i, precision=hp))
    ii = jnp.arange(L)[:, None]
    jj = jnp.arange(L)[None, :]
    kf = jnp.where((jj >= ii)[:, :, None, None, None], kk[:, 0][jnp.clip(jj - ii, 0, L)], 0.0)
    kb = jnp.where((ii >= jj)[:, :, None, None, None], kk[:, 1][jnp.clip(ii - jj, 0, L)], 0.0)
    t_op = jnp.transpose(kf + kb, (2, 0, 4, 1, 3)).reshape(G, S5_W, S5_W)

    def state_in(mr, mi):
        e_re = jnp.transpose(mr, (1, 0, 3, 2)).reshape(G, S5_W, P)
        e_im = jnp.transpose(mi, (1, 0, 3, 2)).reshape(G, S5_W, P)
        return jnp.concatenate([e_re, e_im, e_im, e_re], axis=-1)
    bm = jnp.concatenate([state_in(m_r[L - 1::-1, 0][:L], m_i[L - 1::-1, 0][:L]),
                          state_in(m_r[:L, 1], m_i[:L, 1])], axis=-1)

    def state_out(d, pows):
        pr = pw_r[pows, d]
        pi = pw_i[pows, d]
        g_r = cr[d][None] * pr[:, :, None, :] - ci[d][None] * pi[:, :, None, :]
        g_i = cr[d][None] * pi[:, :, None, :] + ci[d][None] * pr[:, :, None, :]
        rows_re = jnp.transpose(g_r, (1, 3, 0, 2)).reshape(G, P, S5_W)
        rows_im = jnp.transpose(-g_i, (1, 3, 0, 2)).reshape(G, P, S5_W)
        return jnp.concatenate([rows_re, rows_im], axis=1)
    cm = jnp.concatenate([state_out(0, jnp.arange(1, L + 1)), state_out(1, jnp.arange(L, 0, -1))], axis=1)

    def step_mult(d):
        ar = jnp.concatenate([pw_r[L, d]] * 4, axis=-1)
        ai = jnp.concatenate([-pw_i[L, d], pw_i[L, d], pw_i[L, d], -pw_i[L, d]], axis=-1)
        return ar, ai
    arf, aif = step_mult(0)
    arb, aib = step_mult(1)
    a_r = jnp.concatenate([arf, arb], axis=-1)[:, None, :]
    a_i = jnp.concatenate([aif, aib], axis=-1)[:, None, :]
    return t_op.astype(BF16), bm.astype(BF16), cm.astype(BF16), a_r, a_i


S5_LANE_GROUPS = LANES // SSM_GROUP
S5_CAT = S5_CHUNK * LANES


def _s5_kernel(u_ref, t_ref, bm_ref, cm_ref, ar_ref, ai_ref, d_ref, o_ref,
               perm_scr, stage_scr, uall_scr, e_scr, s_scr, *, nc, nb, seg):
    rows = nb * nc
    w2 = S5_W
    p2 = 2 * SSM_STATE

    @pl.when((pl.program_id(0) == 0) & (pl.program_id(1) == 0))
    def _():
        def body(t, carry):
            gc = lax.broadcasted_iota(I32, (LANES, S5_CAT), 0)
            tgt = (gc // SSM_GROUP) * S5_W + t * SSM_GROUP + gc % SSM_GROUP
            col = lax.broadcasted_iota(I32, (LANES, S5_CAT), 1)
            r0 = pl.multiple_of(t * LANES, LANES)
            perm_scr[pl.ds(r0, LANES), :] = jnp.where(col == tgt, 1.0, 0.0).astype(BF16)
            return carry
        lax.fori_loop(0, S5_CHUNK, body, 0)

    def relayout_in(k, carry):
        tok0 = pl.multiple_of(k * (seg * S5_CHUNK), seg * S5_CHUNK)
        stage_scr[...] = u_ref[pl.ds(tok0, seg * S5_CHUNK), :].astype(F32)
        z = jnp.concatenate([stage_scr[pl.ds(t, seg, stride=S5_CHUNK), :] for t in range(S5_CHUNK)], axis=1)
        r0 = pl.multiple_of(k * seg, seg)
        uall_scr[pl.ds(r0, seg), :] = jnp.dot(z.astype(BF16), perm_scr[...],
                                              preferred_element_type=F32).astype(BF16)
        return carry
    lax.fori_loop(0, rows // seg, relayout_in, 0)

    for gp in range(S5_LANE_GROUPS // 2):
        pair = (2 * gp, 2 * gp + 1)
        for q, g in enumerate(pair):
            e = jnp.dot(uall_scr[:, g * w2:(g + 1) * w2], bm_ref[g], preferred_element_type=F32)
            for j in range(4):
                e_scr[4 * q + j] = e[:, j * p2:(j + 1) * p2]
        mult = [[(ar_ref[g, :, j * p2:(j + 1) * p2], ai_ref[g, :, j * p2:(j + 1) * p2]) for j in range(4)]
                for g in pair]

        def body(c, carry):
            rf = pl.ds(c, nb, stride=nc)
            rb = pl.ds(nc - 1 - c, nb, stride=nc)
            out = []
            for q in range(2):
                xf, xfs, zb, zbs = carry[4 * q:4 * q + 4]
                m = mult[q]
                s_scr[2 * q, rf, :] = xf
                s_scr[2 * q + 1, rb, :] = zb
                out += [m[0][0] * xf + m[0][1] * xfs + e_scr[4 * q, rf, :],
                        m[1][0] * xfs + m[1][1] * xf + e_scr[4 * q + 1, rf, :],
                        m[2][0] * zb + m[2][1] * zbs + e_scr[4 * q + 2, rb, :],
                        m[3][0] * zbs + m[3][1] * zb + e_scr[4 * q + 3, rb, :]]
            return tuple(out)
        z0 = jnp.zeros((nb, p2), F32)
        lax.fori_loop(0, nc, body, (z0,) * 8, unroll=2)

        for q, g in enumerate(pair):
            ug = uall_scr[:, g * w2:(g + 1) * w2]
            st = jnp.concatenate([s_scr[2 * q], s_scr[2 * q + 1]], axis=1).astype(BF16)
            y = (jnp.dot(ug, t_ref[g], preferred_element_type=F32)
                 + jnp.dot(st, cm_ref[g], preferred_element_type=F32)
                 + d_ref[g] * ug.astype(F32))
            uall_scr[:, g * w2:(g + 1) * w2] = _gelu_tanh(y).astype(BF16)

    def relayout_out(k, carry):
        r0 = pl.multiple_of(k * seg, seg)
        yt = lax.dot_general(uall_scr[pl.ds(r0, seg), :], perm_scr[...], (((1,), (1,)), ((), ())),
                             preferred_element_type=F32)
        for t in range(S5_CHUNK):
            stage_scr[pl.ds(t, seg, stride=S5_CHUNK), :] = yt[:, t * LANES:(t + 1) * LANES]
        tok0 = pl.multiple_of(k * (seg * S5_CHUNK), seg * S5_CHUNK)
        o_ref[pl.ds(tok0, seg * S5_CHUNK), :] = stage_scr[...].astype(BF16)
        return carry
    lax.fori_loop(0, rows // seg, relayout_out, 0)


def _s5_branch(proj, ops, d_tiled, batch, seq):
    t_op, bm, cm, a_r, a_i = ops
    n = batch * seq
    nc = seq // S5_CHUNK
    rows_blk = min(16384, n)
    nb = rows_blk // seq
    seg = min(256, nb * nc)
    ucol = (3 * ATTN_W) // LANES
    lg = S5_LANE_GROUPS
    blk = lambda gb, sb: (gb, 0, 0)
    return pl.pallas_call(
        functools.partial(_s5_kernel, nc=nc, nb=nb, seg=seg),
        grid=(D_SSM // LANES, n // rows_blk),
        in_specs=[
            pl.BlockSpec((rows_blk, LANES), lambda gb, sb: (sb, ucol + gb)),
            pl.BlockSpec((lg, S5_W, S5_W), blk),
            pl.BlockSpec((lg, S5_W, 2 * S5_W), blk),
            pl.BlockSpec((lg, S5_W, S5_W), blk),
            pl.BlockSpec((lg, 1, 2 * S5_W), blk),
            pl.BlockSpec((lg, 1, 2 * S5_W), blk),
            pl.BlockSpec((lg, 1, S5_W), blk),
        ],
        out_specs=pl.BlockSpec((rows_blk, LANES), lambda gb, sb: (sb, gb)),
        out_shape=jax.ShapeDtypeStruct((n, D_SSM), BF16),
        scratch_shapes=[
            pltpu.VMEM((S5_CAT, S5_CAT), BF16),
            pltpu.VMEM((seg * S5_CHUNK, LANES), F32),
            pltpu.VMEM((nb * nc, S5_CAT), BF16),
            pltpu.VMEM((8, nb * nc, LANES), F32),
            pltpu.VMEM((4, nb * nc, LANES), F32),
        ],
        compiler_params=_cparams(("arbitrary", "arbitrary")),
        name="s5_chunked",
    )(proj, t_op, bm, cm, a_r, a_i, d_tiled)


def _merge_kernel(y_ref, a_ref, gs_ref, ga_ref, wglu_ref, bglu_ref, wap_ref, wsp_ref, o_ref):
    y = y_ref[...]
    z = jnp.dot(y, wglu_ref[...], preferred_element_type=F32) + bglu_ref[...]
    act = (y.astype(F32) * _sigmoid(z)).astype(BF16)
    pa = jnp.dot(a_ref[...], wap_ref[...], preferred_element_type=F32)
    ps = jnp.dot(act, wsp_ref[...], preferred_element_type=F32)
    o = _sigmoid(ga_ref[...].astype(F32)) * pa + _sigmoid(gs_ref[...].astype(F32)) * ps
    o_ref[...] = o.astype(BF16)


def _merge(yact, attn, proj, wglu, bglu, wap, wsp):
    n = yact.shape[0]
    tm = min(256, n)
    gcol = (3 * ATTN_W + D_SSM) // D_MODEL
    const = lambda i: (0, 0)
    return pl.pallas_call(
        _merge_kernel,
        grid=(n // tm,),
        in_specs=[
            pl.BlockSpec((tm, D_SSM), lambda i: (i, 0)),
            pl.BlockSpec((tm, ATTN_W), lambda i: (i, 0)),
            pl.BlockSpec((tm, D_MODEL), lambda i: (i, gcol)),
            pl.BlockSpec((tm, D_MODEL), lambda i: (i, gcol + 1)),
            pl.BlockSpec((D_SSM, D_SSM), const),
            pl.BlockSpec((1, D_SSM), const),
            pl.BlockSpec((ATTN_W, D_MODEL), const),
            pl.BlockSpec((D_SSM, D_MODEL), const),
        ],
        out_specs=pl.BlockSpec((tm, D_MODEL), lambda i: (i, 0)),
        out_shape=jax.ShapeDtypeStruct((n, D_MODEL), BF16),
        compiler_params=_cparams(("parallel",)),
        name="glu_merge",
    )(yact, attn, proj, proj, wglu, bglu, wap, wsp)


def _outproj_kernel(x_ref, m_ref, wout_ref, gin_ref, bin_ref, g1_ref, b1_ref, wr_ref, br_ref,
                    x1_ref, xp_ref, te_ref, tg_ref):
    xn = _ln(x_ref[...], gin_ref[...], bin_ref[...])
    z = DN_ALPHA * xn + jnp.dot(m_ref[...], wout_ref[...], preferred_element_type=F32)
    x1 = _ln(z, g1_ref[...], b1_ref[...])
    x1_ref[...] = x1
    _store_row_tiles(xp_ref, 0, x1.shape[0], _pack_rows(x1))

    x_hi = x1.astype(BF16)
    x_lo = (x1 - x_hi.astype(F32)).astype(BF16)
    r = (jnp.dot(x_hi, wr_ref[...], preferred_element_type=F32)
         + jnp.dot(x_lo, wr_ref[...], preferred_element_type=F32))
    logits = r[:, :N_EXPERTS] + r[:, N_EXPERTS:] + br_ref[...]
    e_iota = lax.broadcasted_iota(I32, logits.shape, 1)
    k_iota = lax.broadcasted_iota(I32, (logits.shape[0], TOP_K), 1)
    vals = jnp.zeros((logits.shape[0], TOP_K), F32)
    idxs = jnp.zeros((logits.shape[0], TOP_K), I32)
    cur = logits
    for k in range(TOP_K):
        m = jnp.max(cur, axis=-1, keepdims=True)
        idx = jnp.min(jnp.where(cur == m, e_iota, N_EXPERTS), axis=-1, keepdims=True)
        vals = jnp.where(k_iota == k, m, vals)
        idxs = jnp.where(k_iota == k, idx, idxs)
        cur = jnp.where(e_iota == idx, -jnp.inf, cur)
    ex = jnp.exp(vals - jnp.max(vals, axis=-1, keepdims=True))
    te_ref[...] = idxs
    tg_ref[...] = ex / jnp.sum(ex, axis=-1, keepdims=True)


def _outproj(x, merged, wout, gin, bin_, g1, b1, wr, br):
    n = x.shape[0]
    tm = min(256, n)
    const = lambda i: (0, 0)
    row = lambda i: (i, 0)
    return pl.pallas_call(
        _outproj_kernel,
        grid=(n // tm,),
        in_specs=[
            pl.BlockSpec((tm, D_MODEL), row),
            pl.BlockSpec((tm, D_MODEL), row),
            pl.BlockSpec((D_MODEL, D_MODEL), const),
            pl.BlockSpec((1, D_MODEL), const),
            pl.BlockSpec((1, D_MODEL), const),
            pl.BlockSpec((1, D_MODEL), const),
            pl.BlockSpec((1, D_MODEL), const),
            pl.BlockSpec((D_MODEL, 2 * N_EXPERTS), const),
            pl.BlockSpec((1, N_EXPERTS), const),
        ],
        out_specs=[
            pl.BlockSpec((tm, D_MODEL), row),
            pl.BlockSpec((tm * ROW_TILE, LANES), row),
            pl.BlockSpec((tm, TOP_K), row),
            pl.BlockSpec((tm, TOP_K), row),
        ],
        out_shape=[
            jax.ShapeDtypeStruct((n, D_MODEL), F32),
            jax.ShapeDtypeStruct((n * ROW_TILE, LANES), U32),
            jax.ShapeDtypeStruct((n, TOP_K), I32),
            jax.ShapeDtypeStruct((n, TOP_K), F32),
        ],
        compiler_params=_cparams(("parallel",)),
        name="outproj_ln_router",
    )(x, merged, wout, gin, bin_, g1, b1, wr, br)


def _gather_rows(idx_smem, base, src_hbm, dst_vmem, sem, count):
    def body(i, carry):
        src_row = pl.multiple_of(idx_smem[base + i], ROW_TILE)
        dst_row = pl.multiple_of(i * ROW_TILE, ROW_TILE)
        pltpu.make_async_copy(src_hbm.at[pl.ds(src_row, ROW_TILE)], dst_vmem.at[pl.ds(dst_row, ROW_TILE)],
                              sem).start()
        return carry
    lax.fori_loop(0, count, body, 0, unroll=8)


def _wait_rows(src_hbm, dst_vmem, sem, count):
    pltpu.make_async_copy(src_hbm.at[pl.ds(0, count * ROW_TILE)], dst_vmem, sem).wait()


def _prefetch_gather(step, nsteps, idx_hbm, idx_smem, idx_sem, src_hbm, gbuf, gsem, count):
    slot = step % 2
    nxt = 1 - slot

    def idx_copy(s, sl):
        return pltpu.make_async_copy(idx_hbm.at[pl.ds(pl.multiple_of(s * count, count), count)],
                                     idx_smem.at[pl.ds(pl.multiple_of(sl * count, count), count)],
                                     idx_sem.at[sl])

    @pl.when(step == 0)
    def _():
        c = idx_copy(0, 0)
        c.start()
        c.wait()
        _gather_rows(idx_smem, 0, src_hbm, gbuf.at[0], gsem.at[0], count)

        @pl.when(nsteps > 1)
        def _():
            idx_copy(1, 1).start()

    _wait_rows(src_hbm, gbuf.at[slot], gsem.at[slot], count)

    @pl.when(step + 1 < nsteps)
    def _():
        idx_copy(step + 1, nxt).wait()
        _gather_rows(idx_smem, nxt * count, src_hbm, gbuf.at[nxt], gsem.at[nxt], count)

    @pl.when(step + 2 < nsteps)
    def _():
        idx_copy(step + 2, slot).start()


def _moe_kernel(be_ref, nu_ref, sh_ref, idx_hbm, x_hbm, wg_ref, bg_ref, wu_ref, bu_ref, wd_ref, bd_ref, o_ref,
                idx_smem, idx_sem, gbuf, gsem, xb_scr, h_scr, *, tmb, rc, nf, nn, tf, tn):
    b = pl.program_id(0)
    t = pl.program_id(1)
    n_used = nu_ref[0]
    active = b < n_used
    slot = b % 2
    nxt = 1 - slot
    per = tmb // (nf + nn)
    win = 2 * tmb

    def idx_copy(s, sl):
        return pltpu.make_async_copy(idx_hbm.at[pl.ds(pl.multiple_of(s * win, win), win)],
                                     idx_smem.at[pl.ds(pl.multiple_of(sl * win, win), win)],
                                     idx_sem.at[sl])

    def gather_next_slice():
        base = nxt * win + sh_ref[b + 1] + t * per
        for i in range(per):
            src_row = pl.multiple_of(idx_smem[base + i], ROW_TILE)
            dst_row = pl.multiple_of((t * per + i) * ROW_TILE, ROW_TILE)
            pltpu.make_async_copy(x_hbm.at[pl.ds(src_row, ROW_TILE)], gbuf.at[nxt, pl.ds(dst_row, ROW_TILE)],
                                  gsem.at[nxt]).start()

    @pl.when((b <= n_used) & (t == 0))
    def _():
        @pl.when(b == 0)
        def _():
            c = idx_copy(0, 0)
            c.start()
            c.wait()
            _gather_rows(idx_smem, sh_ref[0], x_hbm, gbuf.at[0], gsem.at[0], tmb)
            idx_copy(1, 1).start()

        _wait_rows(x_hbm, gbuf.at[slot], gsem.at[slot], tmb)

    @pl.when(active & (t == 0))
    def _():
        idx_copy(b + 1, nxt).wait()

        @pl.when(b + 2 <= n_used)
        def _():
            idx_copy(b + 2, slot).start()

        def body(c, carry):
            r0 = pl.multiple_of(c * rc, rc)
            r = pl.ds(r0, rc)
            for s in range(ROW_TILE):
                c_lo, c_hi = _packed_chunks(s, D_MODEL)
                lo, hi = _unpack_rows(_load_row_tile_words(gbuf.at[slot], r0, rc, s))
                xb_scr[r, c_lo * LANES:(c_lo + 1) * LANES] = lo.astype(BF16)
                xb_scr[r, c_hi * LANES:(c_hi + 1) * LANES] = hi.astype(BF16)
            return carry
        lax.fori_loop(0, tmb // rc, body, 0)

    @pl.when(active & (t < nf))
    def _():
        gather_next_slice()
        x = xb_scr[...]
        g = jnp.dot(x, wg_ref[0], preferred_element_type=F32) + bg_ref[0]
        u = jnp.dot(x, wu_ref[0], preferred_element_type=F32) + bu_ref[0]
        g = jnp.minimum(g, SWIGLU_LIMIT)
        u = jnp.clip(u, -SWIGLU_LIMIT, SWIGLU_LIMIT)
        h = (g * _sigmoid(SWIGLU_ALPHA * g) * (u + 1.0)).astype(BF16)
        for f in range(nf):
            @pl.when(t == f)
            def _():
                h_scr[:, f * tf:(f + 1) * tf] = h

    @pl.when(active & (t >= nf))
    def _():
        gather_next_slice()
        y = jnp.dot(h_scr[...], wd_ref[0], preferred_element_type=F32) + bd_ref[0]
        sub_per_tile = tn // (2 * LANES)
        s0 = (t - nf) * sub_per_tile
        for c in range(tmb // rc):
            yc = y[c * rc:(c + 1) * rc]
            for q in range(sub_per_tile):
                lo = yc[:, q * LANES:(q + 1) * LANES]
                hi = yc[:, tn // 2 + q * LANES:tn // 2 + (q + 1) * LANES]
                o_ref[pl.ds(c * rc * ROW_TILE + s0 + q, rc, stride=ROW_TILE), :] = _pack_words(lo, hi)

    @pl.when(jnp.logical_not(active) & (t == nf + nn - 1))
    def _():
        o_ref[...] = jnp.zeros(o_ref.shape, U32)


def _moe_experts(block_e, n_used, shift, row_win, x_packed, wg, bg, wu, bu, wd, bd, tmb):
    nblk = block_e.shape[0]
    tf = MOE_TILE
    tn = MOE_TILE
    nf = D_FF // tf
    nn = D_MODEL // tn
    rc = min(256, tmb)

    def fsel(b, t, nu):
        return jnp.where(b < nu[0], jnp.minimum(t, nf - 1), nf - 1)

    def nsel(b, t, nu):
        return jnp.where(b < nu[0], jnp.maximum(t - nf, 0), nn - 1)

    grid_spec = pltpu.PrefetchScalarGridSpec(
        num_scalar_prefetch=3,
        grid=(nblk, nf + nn),
        in_specs=[
            pl.BlockSpec(memory_space=pl.ANY),
            pl.BlockSpec(memory_space=pl.ANY),
            pl.BlockSpec((1, D_MODEL, tf), lambda b, t, be, nu, sh: (be[b], 0, fsel(b, t, nu))),
            pl.BlockSpec((1, 1, tf), lambda b, t, be, nu, sh: (be[b], 0, fsel(b, t, nu))),
            pl.BlockSpec((1, D_MODEL, tf), lambda b, t, be, nu, sh: (be[b], 0, fsel(b, t, nu))),
            pl.BlockSpec((1, 1, tf), lambda b, t, be, nu, sh: (be[b], 0, fsel(b, t, nu))),
            pl.BlockSpec((1, D_FF, tn), lambda b, t, be, nu, sh: (be[b], 0, nsel(b, t, nu))),
            pl.BlockSpec((1, 1, tn), lambda b, t, be, nu, sh: (be[b], 0, nsel(b, t, nu))),
        ],
        out_specs=pl.BlockSpec((tmb * ROW_TILE, LANES), lambda b, t, be, nu, sh: (b, 0)),
        scratch_shapes=[
            pltpu.SMEM((4 * tmb,), I32),
            pltpu.SemaphoreType.DMA((2,)),
            pltpu.VMEM((2, tmb * ROW_TILE, LANES), U32),
            pltpu.SemaphoreType.DMA((2,)),
            pltpu.VMEM((tmb, D_MODEL), BF16),
            pltpu.VMEM((tmb, D_FF), BF16),
        ],
    )
    return pl.pallas_call(
        functools.partial(_moe_kernel, tmb=tmb, rc=rc, nf=nf, nn=nn, tf=tf, tn=tn),
        grid_spec=grid_spec,
        out_shape=jax.ShapeDtypeStruct((nblk * tmb * ROW_TILE, LANES), U32),
        compiler_params=_cparams(("arbitrary", "arbitrary"), disable_bounds_checks=True),
        name="moe_experts",
    )(block_e, n_used, shift, row_win, x_packed, wg, bg, wu, bu, wd, bd)


def _combine_kernel(idx_hbm, y_hbm, x1_ref, gate_ref, g_ref, b_ref, o_ref,
                    idx_smem, idx_sem, gbuf, gsem, *, tc):
    i = pl.program_id(0)
    _prefetch_gather(i, pl.num_programs(0), idx_hbm, idx_smem, idx_sem, y_hbm, gbuf, gsem, TOP_K * tc)
    slot = i % 2
    gates = gate_ref[...]
    gk = [gates[:, k:k + 1] for k in range(TOP_K)]
    ncol = D_MODEL // LANES
    z = [DN_ALPHA * x1_ref[:, c * LANES:(c + 1) * LANES] for c in range(ncol)]
    for s in range(ROW_TILE):
        c_lo, c_hi = _packed_chunks(s, MOE_TILE)
        for k in range(TOP_K):
            lo, hi = _unpack_rows(_load_row_tile_words(gbuf.at[slot], k * tc, tc, s))
            z[c_lo] = z[c_lo] + gk[k] * lo
            z[c_hi] = z[c_hi] + gk[k] * hi
    mu = sum(jnp.sum(zc, axis=-1, keepdims=True) for zc in z) / D_MODEL
    z = [zc - mu for zc in z]
    var = sum(jnp.sum(zc * zc, axis=-1, keepdims=True) for zc in z) / D_MODEL
    inv = lax.rsqrt(var + LN_EPS)
    for c in range(ncol):
        cols = slice(c * LANES, (c + 1) * LANES)
        o_ref[:, cols] = z[c] * inv * g_ref[:, cols] + b_ref[:, cols]


def _combine(dest_tiles, yb, x1, gates, g2, b2, tc):
    n = x1.shape[0]
    const = lambda i: (0, 0)
    row = lambda i: (i, 0)
    return pl.pallas_call(
        functools.partial(_combine_kernel, tc=tc),
        grid=(n // tc,),
        in_specs=[
            pl.BlockSpec(memory_space=pl.ANY),
            pl.BlockSpec(memory_space=pl.ANY),
            pl.BlockSpec((tc, D_MODEL), row),
            pl.BlockSpec((tc, TOP_K), row),
            pl.BlockSpec((1, D_MODEL), const),
            pl.BlockSpec((1, D_MODEL), const),
        ],
        out_specs=pl.BlockSpec((tc, D_MODEL), row),
        out_shape=jax.ShapeDtypeStruct((n, D_MODEL), F32),
        scratch_shapes=[
            pltpu.SMEM((2 * TOP_K * tc,), I32),
            pltpu.SemaphoreType.DMA((2,)),
            pltpu.VMEM((2, TOP_K * tc * ROW_TILE, LANES), U32),
            pltpu.SemaphoreType.DMA((2,)),
        ],
        compiler_params=_cparams(("arbitrary",), disable_bounds_checks=True),
        name="moe_combine_ln",
    )(dest_tiles, yb, x1, gates, g2, b2)


def _route_meta(top_e, tmb):
    nt = top_e.shape[0]
    nblk = (nt * TOP_K) // tmb + N_EXPERTS
    e_ids = jnp.arange(N_EXPERTS, dtype=I32)
    sel = top_e[:, :, None] == e_ids[None, None, :]
    onehot = sel.sum(axis=1).astype(I32)
    incl = jnp.cumsum(onehot, axis=0)
    counts = incl[-1]
    padded = (counts + tmb - 1) // tmb * tmb
    pend = jnp.cumsum(padded)
    pstart = pend - padded
    start = jnp.cumsum(counts) - counts
    dest = jnp.where(sel, (pstart[None, :] + incl - onehot)[:, None, :], 0).sum(axis=-1).astype(I32)
    n_used = (pend[-1] // tmb).astype(I32)
    blk = jnp.arange(nblk, dtype=I32)
    be = jnp.clip(jnp.searchsorted(pend, blk * tmb, side='right'), 0, N_EXPERTS - 1).astype(I32)
    be = jnp.where(blk < n_used, be, be[jnp.maximum(n_used - 1, 0)])
    order = jnp.argsort(top_e.reshape(-1), stable=True).astype(I32)
    lines = (nt * TOP_K) // LANES + 2 * tmb // LANES + 1
    tok_sorted = jnp.zeros((lines * LANES,), I32).at[:nt * TOP_K].set((order // TOP_K) * ROW_TILE)
    win = jnp.clip(blk * tmb - pstart[be] + start[be], 0, nt * TOP_K)
    line_idx = (win // LANES)[:, None] + jnp.arange(2 * tmb // LANES, dtype=I32)[None, :]
    row_win = jnp.take(tok_sorted.reshape(lines, LANES), line_idx, axis=0).reshape(-1)
    shift = (win % LANES).astype(I32)
    return be, n_used.reshape(1), shift, row_win, dest * ROW_TILE


def kernel(x_prompt, x_sample, ln_in_g, ln_in_b, w_in, lam_q1, lam_k1, lam_q2, lam_k2, subln_g, w_attn_proj, ssm_lam_re, ssm_lam_im, ssm_log_dt, ssm_b_re, ssm_b_im, ssm_c_re, ssm_c_im, ssm_d, w_glu, b_glu, w_ssm_proj, w_out, ln1_g, ln1_b, w_router, b_router, w_gate, b_gate, w_up, b_up, w_down, b_down, ln2_g, ln2_b):
    l = 0
    row = lambda v: v.reshape(1, -1).astype(F32)
    lambda_init = 0.8 - 0.6 * math.exp(-0.3 * l)
    lam = (jnp.exp(jnp.sum(lam_q1[l].astype(F32) * lam_k1[l].astype(F32)))
           - jnp.exp(jnp.sum(lam_q2[l].astype(F32) * lam_k2[l].astype(F32))) + lambda_init).reshape(1, 1)
    w_in_b = w_in[l].astype(BF16)
    wglu_b = w_glu[l].astype(BF16)
    wap_b = w_attn_proj[l].astype(BF16)
    wsp_b = w_ssm_proj[l].astype(BF16)
    wout_b = w_out[l].astype(BF16)
    wg_b = w_gate[l].astype(BF16)
    wu_b = w_up[l].astype(BF16)
    wd_b = w_down[l].astype(BF16)
    wr = w_router[l].astype(F32)
    wr_hi = wr.astype(BF16)
    wr_split = jnp.concatenate([wr_hi, (wr - wr_hi.astype(F32)).astype(BF16)], axis=1)
    s5_ops = _s5_operators(ssm_lam_re[l], ssm_lam_im[l], ssm_log_dt[l], ssm_b_re[l], ssm_b_im[l],
                           ssm_c_re[l], ssm_c_im[l])
    d_tiled = jnp.tile(ssm_d[l].astype(F32).reshape(SSM_GROUPS, 1, SSM_GROUP), (1, 1, S5_CHUNK))

    def pre_moe(x3):
        batch, seq, _ = x3.shape
        x = x3.reshape(batch * seq, D_MODEL)
        proj = _inproj(x, row(ln_in_g), row(ln_in_b), w_in_b)
        cos_t, sin_t = _rope_tables(seq)
        attn = _attention(proj, lam, cos_t, sin_t, row(subln_g[l]), batch, seq, lambda_init)
        yact = _s5_branch(proj, s5_ops, d_tiled, batch, seq)
        merged = _merge(yact, attn, proj, wglu_b, row(b_glu[l]), wap_b, wsp_b)
        return _outproj(x, merged, wout_b, row(ln_in_g), row(ln_in_b), row(ln1_g[l]), row(ln1_b[l]),
                        wr_split, row(b_router[l]))

    parts = [pre_moe(x_prompt), pre_moe(x_sample)]
    xp_all = jnp.concatenate([p[1] for p in parts], axis=0)
    te_all = jnp.concatenate([p[2] for p in parts], axis=0)
    nt = xp_all.shape[0]
    tmb = 1024 if nt >= 8192 else 128
    be, n_used, shift, row_win, dest = _route_meta(te_all, tmb)
    yb = _moe_experts(be, n_used, shift, row_win, xp_all, wg_b, b_gate[l].astype(F32)[:, None, :],
                      wu_b, b_up[l].astype(F32)[:, None, :], wd_b, b_down[l].astype(F32)[:, None, :], tmb)

    outs = []
    off = 0
    for x3, p in zip((x_prompt, x_sample), parts):
        n = p[0].shape[0]
        tc = min(256, n)
        d = dest[off:off + n].reshape(n // tc, tc, TOP_K)
        d = jnp.transpose(d, (0, 2, 1)).reshape(-1)
        out = _combine(d, yb, p[0], p[3], row(ln2_g[l]), row(ln2_b[l]), tc)
        outs.append(out.reshape(x3.shape))
        off += n
    return tuple(outs)
```

```python
import functools
import math

import jax
import jax.numpy as jnp
from jax import lax
from jax.experimental import pallas as pl
from jax.experimental.pallas import tpu as pltpu

F32 = jnp.float32
BF16 = jnp.bfloat16
U32 = jnp.uint32
I32 = jnp.int32

D_MODEL = 2048
DEPTH = 1
N_HEADS = 8
HEAD_DIM = 64
HEAD_W = 2 * HEAD_DIM
ATTN_W = N_HEADS * HEAD_W
ROT_DIM = HEAD_DIM // 4
ROPE_THETA = 500000.0
SSM_GROUP = 16
SSM_GROUPS = 64
D_SSM = SSM_GROUP * SSM_GROUPS
SSM_STATE = 64
S5_CHUNK = 16
S5_W = S5_CHUNK * SSM_GROUP
N_IN = 3 * ATTN_W + D_SSM + 2 * D_MODEL
N_EXPERTS = 32
TOP_K = 4
D_FF = D_MODEL
SWIGLU_LIMIT = 7.0
SWIGLU_ALPHA = 1.702
DN_ALPHA = (2.0 * DEPTH) ** 0.25
LN_EPS = 1e-5
HALF = D_MODEL // 2
LANES = 128
ROW_TILE = 8
MOE_TILE = 512

VMEM_LIMIT = 56 * 1024 * 1024


def _cparams(sem, **kw):
    return pltpu.CompilerParams(dimension_semantics=sem, vmem_limit_bytes=VMEM_LIMIT, **kw)


def _ln(x, g, b):
    mu = jnp.mean(x, axis=-1, keepdims=True)
    xc = x - mu
    var = jnp.mean(xc * xc, axis=-1, keepdims=True)
    return xc * lax.rsqrt(var + LN_EPS) * g + b


def _sigmoid(x):
    return 1.0 / (1.0 + jnp.exp(-x))


def _gelu_tanh(x):
    return 0.5 * x * (1.0 + jnp.tanh(math.sqrt(2.0 / math.pi) * (x + 0.044715 * (x * x * x))))


def _pack_words(lo, hi):
    lo = lax.bitcast_convert_type(lo.astype(BF16).astype(F32), U32)
    hi = lax.bitcast_convert_type(hi.astype(BF16).astype(F32), U32)
    return (hi & jnp.uint32(0xFFFF0000)) | (lo >> 16)


def _pack_rows(x):
    return _pack_words(x[:, :HALF], x[:, HALF:])


def _packed_chunks(s, tile):
    sub_per_tile = tile // (2 * LANES)
    n, q = divmod(s, sub_per_tile)
    lo = n * (tile // LANES) + q
    return lo, lo + sub_per_tile


def _unpack_rows(w):
    lo = lax.bitcast_convert_type(w << 16, F32)
    hi = lax.bitcast_convert_type(w & jnp.uint32(0xFFFF0000), F32)
    return lo, hi


def _store_row_tiles(ref, r0, nrows, packed):
    for s in range(ROW_TILE):
        ref[pl.ds(r0 * ROW_TILE + s, nrows, stride=ROW_TILE), :] = packed[:, s * LANES:(s + 1) * LANES]


def _load_row_tile_words(ref, r0, nrows, s):
    return ref[pl.ds(r0 * ROW_TILE + s, nrows, stride=ROW_TILE), :]


def _inproj_kernel(x_ref, g_ref, b_ref, w_ref, o_ref, xn_ref, *, tm, rc):
    @pl.when(pl.program_id(1) == 0)
    def _():
        def body(c, carry):
            r = pl.ds(pl.multiple_of(c * rc, rc), rc)
            xn_ref[r, :] = _ln(x_ref[r, :], g_ref[...], b_ref[...]).astype(BF16)
            return carry
        lax.fori_loop(0, tm // rc, body, 0)

    o_ref[...] = jnp.dot(xn_ref[...], w_ref[...], preferred_element_type=F32).astype(BF16)


def _inproj(x, g, b, w_bf16):
    n = x.shape[0]
    tm = min(1024, n)
    tn = 1024
    rc = min(256, tm)
    return pl.pallas_call(
        functools.partial(_inproj_kernel, tm=tm, rc=rc),
        grid=(n // tm, N_IN // tn),
        in_specs=[
            pl.BlockSpec((tm, D_MODEL), lambda i, j: (i, 0)),
            pl.BlockSpec((1, D_MODEL), lambda i, j: (0, 0)),
            pl.BlockSpec((1, D_MODEL), lambda i, j: (0, 0)),
            pl.BlockSpec((D_MODEL, tn), lambda i, j: (0, j)),
        ],
        out_specs=pl.BlockSpec((tm, tn), lambda i, j: (i, j)),
        out_shape=jax.ShapeDtypeStruct((n, N_IN), BF16),
        scratch_shapes=[pltpu.VMEM((tm, D_MODEL), BF16)],
        compiler_params=_cparams(("parallel", "arbitrary")),
        name="ln_inproj",
    )(x, g, b, w_bf16)


def _rope(x, c, s):
    lane = lax.broadcasted_iota(I32, (1, HEAD_W), 1)
    first_half = (lane % HEAD_DIM) < (ROT_DIM // 2)
    xs = jnp.where(first_half, pltpu.roll(x, HEAD_W - ROT_DIM // 2, 1), pltpu.roll(x, ROT_DIM // 2, 1))
    return x * c + xs * s


ONES_ROWS = 16


def _attn_kernel(lam_ref, q_ref, k_ref, v_ref, cq_ref, sq_ref, ck_ref, sk_ref, g_ref, o_ref,
                 k_scr, vt_scr, q_scr, *, seq, tq, tk, out_scale):
    nkv = seq // tk

    @pl.when(pl.program_id(2) == 0)
    def _():
        def body(c, carry):
            r = pl.ds(pl.multiple_of(c * tk, tk), tk)
            k_scr[r, :] = _rope(k_ref[r, :].astype(F32), ck_ref[r, :], sk_ref[r, :]).astype(BF16)
            vt_scr[c, :HEAD_W, :] = v_ref[r, :].astype(F32).T.astype(BF16)
            vt_scr[c, HEAD_W:, :] = jnp.ones((ONES_ROWS, tk), BF16)
            return carry
        lax.fori_loop(0, nkv, body, 0)

    q = _rope(q_ref[...].astype(F32), cq_ref[...], sq_ref[...]) * (HEAD_DIM ** -0.5 * math.log2(math.e))
    lane = lax.broadcasted_iota(I32, (1, HEAD_W), 1)
    comp1 = lane < HEAD_DIM
    q_scr[pl.ds(0, tq), :] = jnp.where(comp1, q, 0.0).astype(BF16)
    q_scr[pl.ds(tq, tq), :] = jnp.where(comp1, 0.0, q).astype(BF16)

    def scores(j):
        return lax.dot_general(k_scr[j * tk:(j + 1) * tk, :], q_scr[...], (((1,), (1,)), ((), ())),
                               preferred_element_type=F32)

    m = jnp.full((1, 2 * tq), -jnp.inf, F32)
    acc = jnp.zeros((HEAD_W + ONES_ROWS, 2 * tq), F32)
    s_cur = scores(0)
    for j in range(nkv):
        s_next = scores(j + 1) if j + 1 < nkv else None
        m_new = jnp.maximum(m, jnp.max(s_cur, axis=0, keepdims=True))
        p = jnp.exp2(s_cur - m_new).astype(BF16)
        acc = jnp.exp2(m - m_new) * acc + jnp.dot(vt_scr[j], p, preferred_element_type=F32)
        m = m_new
        s_cur = s_next

    o = acc[:HEAD_W] / acc[HEAD_W:HEAD_W + 1]
    o = o[:, :tq] - lam_ref[0, 0] * o[:, tq:]
    o = o * lax.rsqrt(jnp.mean(o * o, axis=0, keepdims=True) + LN_EPS) * g_ref[...] * out_scale
    o_ref[...] = o.T.astype(BF16)


def _attention(proj, lam, cos_t, sin_t, subln_g, batch, seq, lambda_init):
    tq = min(512, seq)
    tk = min(1024, seq)
    nq = seq // tq
    kern = functools.partial(_attn_kernel, seq=seq, tq=tq, tk=tk, out_scale=1.0 - lambda_init)
    return pl.pallas_call(
        kern,
        grid=(batch, N_HEADS, nq),
        in_specs=[
            pl.BlockSpec(memory_space=pltpu.SMEM),
            pl.BlockSpec((tq, HEAD_W), lambda b, h, i: (b * nq + i, h)),
            pl.BlockSpec((seq, HEAD_W), lambda b, h, i: (b, N_HEADS + h)),
            pl.BlockSpec((seq, HEAD_W), lambda b, h, i: (b, 2 * N_HEADS + h)),
            pl.BlockSpec((tq, HEAD_W), lambda b, h, i: (i, 0)),
            pl.BlockSpec((tq, HEAD_W), lambda b, h, i: (i, 0)),
            pl.BlockSpec((seq, HEAD_W), lambda b, h, i: (0, 0)),
            pl.BlockSpec((seq, HEAD_W), lambda b, h, i: (0, 0)),
            pl.BlockSpec((HEAD_W, 1), lambda b, h, i: (0, 0)),
        ],
        out_specs=pl.BlockSpec((tq, HEAD_W), lambda b, h, i: (b * nq + i, h)),
        out_shape=jax.ShapeDtypeStruct((batch * seq, ATTN_W), BF16),
        scratch_shapes=[
            pltpu.VMEM((seq, HEAD_W), BF16),
            pltpu.VMEM((seq // tk, HEAD_W + ONES_ROWS, tk), BF16),
            pltpu.VMEM((2 * tq, HEAD_W), BF16),
        ],
        compiler_params=_cparams(("parallel", "parallel", "arbitrary")),
        name="diff_attention",
    )(lam, proj, proj, proj, cos_t, sin_t, cos_t, sin_t, subln_g.reshape(HEAD_W, 1))


def _rope_tables(seq):
    half = ROT_DIM // 2
    inv = ROPE_THETA ** (-jnp.arange(0, ROT_DIM, 2, dtype=F32) / ROT_DIM)
    ang = jnp.arange(seq, dtype=F32)[:, None] * inv[None, :]
    lane = jnp.arange(HEAD_W) % HEAD_DIM
    cos_l = jnp.cos(ang)[:, lane % half]
    sin_l = jnp.sin(ang)[:, lane % half]
    cos_t = jnp.where(lane[None, :] < ROT_DIM, cos_l, 1.0)
    sin_t = jnp.where(lane[None, :] < half, -sin_l, jnp.where(lane[None, :] < ROT_DIM, sin_l, 0.0))
    return cos_t.astype(F32), sin_t.astype(F32)


def _s5_operators(lam_re, lam_im, log_dt, b_re, b_im, c_re, c_im):
    L, G, P, CG = S5_CHUNK, SSM_GROUPS, SSM_STATE, SSM_GROUP
    hp = lax.Precision.HIGHEST
    lr = jnp.minimum(lam_re.astype(F32), -1e-4)
    li = lam_im.astype(F32)
    dt = jnp.exp(log_dt.astype(F32))[..., None]
    nn = jnp.arange(L + 1, dtype=F32)[:, None, None, None]
    mag = jnp.exp(nn * (lr * dt)[None])
    pw_r = mag * jnp.cos(nn * (li * dt)[None])
    pw_i = mag * jnp.sin(nn * (li * dt)[None])
    ab_r, ab_i = pw_r[1], pw_i[1]
    nr = ab_r - 1.0
    den = lr * lr + li * li
    cr_ = ((nr * lr + ab_i * li) / den)[..., None]
    ci_ = ((ab_i * lr - nr * li) / den)[..., None]
    br = b_re.astype(F32)
    bi = b_im.astype(F32)
    bb_r = cr_ * br - ci_ * bi
    bb_i = cr_ * bi + ci_ * br
    cr = c_re.astype(F32)
    ci = c_im.astype(F32)

    m_r = pw_r[..., None] * bb_r[None] - pw_i[..., None] * bb_i[None]
    m_i = pw_r[..., None] * bb_i[None] + pw_i[..., None] * bb_r[None]
    kk = (jnp.einsum('dgop,ndgpi->ndgoi', cr, m_r, precision=hp)
          - jnp.einsum('dgop,ndgpi->ndgoi', ci, m_i, precision=hp))
    ii = jnp.arange(L)[:, None]
    jj = jnp.arange(L)[None, :]
    kf = jnp.where((jj >= ii)[:, :, None, None, None], kk[:, 0][jnp.clip(jj - ii, 0, L)], 0.0)
    kb = jnp.where((ii >= jj)[:, :, None, None, None], kk[:, 1][jnp.clip(ii - jj, 0, L)], 0.0)
    t_op = jnp.transpose(kf + kb, (2, 0, 4, 1, 3)).reshape(G, S5_W, S5_W)

    def state_in(mr, mi):
        e_re = jnp.transpose(mr, (1, 0, 3, 2)).reshape(G, S5_W, P)
        e_im = jnp.transpose(mi, (1, 0, 3, 2)).reshape(G, S5_W, P)
        return jnp.concatenate([e_re, e_im, e_im, e_re], axis=-1)
    bm = jnp.concatenate([state_in(m_r[L - 1::-1, 0][:L], m_i[L - 1::-1, 0][:L]),
                          state_in(m_r[:L, 1], m_i[:L, 1])], axis=-1)

    def state_out(d, pows):
        pr = pw_r[pows, d]
        pi = pw_i[pows, d]
        g_r = cr[d][None] * pr[:, :, None, :] - ci[d][None] * pi[:, :, None, :]
        g_i = cr[d][None] * pi[:, :, None, :] + ci[d][None] * pr[:, :, None, :]
        rows_re = jnp.transpose(g_r, (1, 3, 0, 2)).reshape(G, P, S5_W)
        rows_im = jnp.transpose(-g_i, (1, 3, 0, 2)).reshape(G, P, S5_W)
        return jnp.concatenate([rows_re, rows_im], axis=1)
    cm = jnp.concatenate([state_out(0, jnp.arange(1, L + 1)), state_out(1, jnp.arange(L, 0, -1))], axis=1)

    def step_mult(d):
        ar = jnp.concatenate([pw_r[L, d]] * 4, axis=-1)
        ai = jnp.concatenate([-pw_i[L, d], pw_i[L, d], pw_i[L, d], -pw_i[L, d]], axis=-1)
        return ar, ai
    arf, aif = step_mult(0)
    arb, aib = step_mult(1)
    a_r = jnp.concatenate([arf, arb], axis=-1)[:, None, :]
    a_i = jnp.concatenate([aif, aib], axis=-1)[:, None, :]
    return t_op.astype(BF16), bm.astype(BF16), cm.astype(BF16), a_r, a_i


S5_LANE_GROUPS = LANES // SSM_GROUP
S5_CAT = S5_CHUNK * LANES


def _s5_kernel(u_ref, t_ref, bm_ref, cm_ref, ar_ref, ai_ref, d_ref, o_ref,
               perm_scr, stage_scr, uall_scr, e_scr, s_scr, *, nc, nb, seg):
    rows = nb * nc
    w2 = S5_W
    p2 = 2 * SSM_STATE

    @pl.when((pl.program_id(0) == 0) & (pl.program_id(1) == 0))
    def _():
        def body(t, carry):
            gc = lax.broadcasted_iota(I32, (LANES, S5_CAT), 0)
            tgt = (gc // SSM_GROUP) * S5_W + t * SSM_GROUP + gc % SSM_GROUP
            col = lax.broadcasted_iota(I32, (LANES, S5_CAT), 1)
            r0 = pl.multiple_of(t * LANES, LANES)
            perm_scr[pl.ds(r0, LANES), :] = jnp.where(col == tgt, 1.0, 0.0).astype(BF16)
            return carry
        lax.fori_loop(0, S5_CHUNK, body, 0)

    def relayout_in(k, carry):
        tok0 = pl.multiple_of(k * (seg * S5_CHUNK), seg * S5_CHUNK)
        stage_scr[...] = u_ref[pl.ds(tok0, seg * S5_CHUNK), :].astype(F32)
        z = jnp.concatenate([stage_scr[pl.ds(t, seg, stride=S5_CHUNK), :] for t in range(S5_CHUNK)], axis=1)
        r0 = pl.multiple_of(k * seg, seg)
        uall_scr[pl.ds(r0, seg), :] = jnp.dot(z.astype(BF16), perm_scr[...],
                                              preferred_element_type=F32).astype(BF16)
        return carry
    lax.fori_loop(0, rows // seg, relayout_in, 0)

    for gp in range(S5_LANE_GROUPS // 2):
        pair = (2 * gp, 2 * gp + 1)
        for q, g in enumerate(pair):
            e = jnp.dot(uall_scr[:, g * w2:(g + 1) * w2], bm_ref[g], preferred_element_type=F32)
            for j in range(4):
                e_scr[4 * q + j] = e[:, j * p2:(j + 1) * p2]
        mult = [[(ar_ref[g, :, j * p2:(j + 1) * p2], ai_ref[g, :, j * p2:(j + 1) * p2]) for j in range(4)]
                for g in pair]

        def body(c, carry):
            rf = pl.ds(c, nb, stride=nc)
            rb = pl.ds(nc - 1 - c, nb, stride=nc)
            out = []
            for q in range(2):
                xf, xfs, zb, zbs = carry[4 * q:4 * q + 4]
                m = mult[q]
                s_scr[2 * q, rf, :] = xf
                s_scr[2 * q + 1, rb, :] = zb
                out += [m[0][0] * xf + m[0][1] * xfs + e_scr[4 * q, rf, :],
                        m[1][0] * xfs + m[1][1] * xf + e_scr[4 * q + 1, rf, :],
                        m[2][0] * zb + m[2][1] * zbs + e_scr[4 * q + 2, rb, :],
                        m[3][0] * zbs + m[3][1] * zb + e_scr[4 * q + 3, rb, :]]
            return tuple(out)
        z0 = jnp.zeros((nb, p2), F32)
        lax.fori_loop(0, nc, body, (z0,) * 8, unroll=2)

        for q, g in enumerate(pair):
            ug = uall_scr[:, g * w2:(g + 1) * w2]
            st = jnp.concatenate([s_scr[2 * q], s_scr[2 * q + 1]], axis=1).astype(BF16)
            y = (jnp.dot(ug, t_ref[g], preferred_element_type=F32)
                 + jnp.dot(st, cm_ref[g], preferred_element_type=F32)
                 + d_ref[g] * ug.astype(F32))
            uall_scr[:, g * w2:(g + 1) * w2] = _gelu_tanh(y).astype(BF16)

    def relayout_out(k, carry):
        r0 = pl.multiple_of(k * seg, seg)
        yt = lax.dot_general(uall_scr[pl.ds(r0, seg), :], perm_scr[...], (((1,), (1,)), ((), ())),
                             preferred_element_type=F32)
        for t in range(S5_CHUNK):
            stage_scr[pl.ds(t, seg, stride=S5_CHUNK), :] = yt[:, t * LANES:(t + 1) * LANES]
        tok0 = pl.multiple_of(k * (seg * S5_CHUNK), seg * S5_CHUNK)
        o_ref[pl.ds(tok0, seg * S5_CHUNK), :] = stage_scr[...].astype(BF16)
        return carry
    lax.fori_loop(0, rows // seg, relayout_out, 0)


def _s5_branch(proj, ops, d_tiled, batch, seq):
    t_op, bm, cm, a_r, a_i = ops
    n = batch * seq
    nc = seq // S5_CHUNK
    rows_blk = min(16384, n)
    nb = rows_blk // seq
    seg = min(256, nb * nc)
    ucol = (3 * ATTN_W) // LANES
    lg = S5_LANE_GROUPS
    blk = lambda gb, sb: (gb, 0, 0)
    return pl.pallas_call(
        functools.partial(_s5_kernel, nc=nc, nb=nb, seg=seg),
        grid=(D_SSM // LANES, n // rows_blk),
        in_specs=[
            pl.BlockSpec((rows_blk, LANES), lambda gb, sb: (sb, ucol + gb)),
            pl.BlockSpec((lg, S5_W, S5_W), blk),
            pl.BlockSpec((lg, S5_W, 2 * S5_W), blk),
            pl.BlockSpec((lg, S5_W, S5_W), blk),
            pl.BlockSpec((lg, 1, 2 * S5_W), blk),
            pl.BlockSpec((lg, 1, 2 * S5_W), blk),
            pl.BlockSpec((lg, 1, S5_W), blk),
        ],
        out_specs=pl.BlockSpec((rows_blk, LANES), lambda gb, sb: (sb, gb)),
        out_shape=jax.ShapeDtypeStruct((n, D_SSM), BF16),
        scratch_shapes=[
            pltpu.VMEM((S5_CAT, S5_CAT), BF16),
            pltpu.VMEM((seg * S5_CHUNK, LANES), F32),
            pltpu.VMEM((nb * nc, S5_CAT), BF16),
            pltpu.VMEM((8, nb * nc, LANES), F32),
            pltpu.VMEM((4, nb * nc, LANES), F32),
        ],
        compiler_params=_cparams(("arbitrary", "arbitrary")),
        name="s5_chunked",
    )(proj, t_op, bm, cm, a_r, a_i, d_tiled)


def _merge_kernel(y_ref, a_ref, gs_ref, ga_ref, wglu_ref, bglu_ref, wap_ref, wsp_ref, o_ref):
    y = y_ref[...]
    z = jnp.dot(y, wglu_ref[...], preferred_element_type=F32) + bglu_ref[...]
    act = (y.astype(F32) * _sigmoid(z)).astype(BF16)
    pa = jnp.dot(a_ref[...], wap_ref[...], preferred_element_type=F32)
    ps = jnp.dot(act, wsp_ref[...], preferred_element_type=F32)
    o = _sigmoid(ga_ref[...].astype(F32)) * pa + _sigmoid(gs_ref[...].astype(F32)) * ps
    o_ref[...] = o.astype(BF16)


def _merge(yact, attn, proj, wglu, bglu, wap, wsp):
    n = yact.shape[0]
    tm = min(256, n)
    gcol = (3 * ATTN_W + D_SSM) // D_MODEL
    const = lambda i: (0, 0)
    return pl.pallas_call(
        _merge_kernel,
        grid=(n // tm,),
        in_specs=[
            pl.BlockSpec((tm, D_SSM), lambda i: (i, 0)),
            pl.BlockSpec((tm, ATTN_W), lambda i: (i, 0)),
            pl.BlockSpec((tm, D_MODEL), lambda i: (i, gcol)),
            pl.BlockSpec((tm, D_MODEL), lambda i: (i, gcol + 1)),
            pl.BlockSpec((D_SSM, D_SSM), const),
            pl.BlockSpec((1, D_SSM), const),
            pl.BlockSpec((ATTN_W, D_MODEL), const),
            pl.BlockSpec((D_SSM, D_MODEL), const),
        ],
        out_specs=pl.BlockSpec((tm, D_MODEL), lambda i: (i, 0)),
        out_shape=jax.ShapeDtypeStruct((n, D_MODEL), BF16),
        compiler_params=_cparams(("parallel",)),
        name="glu_merge",
    )(yact, attn, proj, proj, wglu, bglu, wap, wsp)


def _outproj_kernel(x_ref, m_ref, wout_ref, gin_ref, bin_ref, g1_ref, b1_ref, wr_ref, br_ref,
                    x1_ref, xp_ref, te_ref, tg_ref):
    xn = _ln(x_ref[...], gin_ref[...], bin_ref[...])
    z = DN_ALPHA * xn + jnp.dot(m_ref[...], wout_ref[...], preferred_element_type=F32)
    x1 = _ln(z, g1_ref[...], b1_ref[...])
    x1_ref[...] = x1
    _store_row_tiles(xp_ref, 0, x1.shape[0], _pack_rows(x1))

    x_hi = x1.astype(BF16)
    x_lo = (x1 - x_hi.astype(F32)).astype(BF16)
    r = (jnp.dot(x_hi, wr_ref[...], preferred_element_type=F32)
         + jnp.dot(x_lo, wr_ref[...], preferred_element_type=F32))
    logits = r[:, :N_EXPERTS] + r[:, N_EXPERTS:] + br_ref[...]
    e_iota = lax.broadcasted_iota(I32, logits.shape, 1)
    k_iota = lax.broadcasted_iota(I32, (logits.shape[0], TOP_K), 1)
    vals = jnp.zeros((logits.shape[0], TOP_K), F32)
    idxs = jnp.zeros((logits.shape[0], TOP_K), I32)
    cur = logits
    for k in range(TOP_K):
        m = jnp.max(cur, axis=-1, keepdims=True)
        idx = jnp.min(jnp.where(cur == m, e_iota, N_EXPERTS), axis=-1, keepdims=True)
        vals = jnp.where(k_iota == k, m, vals)
        idxs = jnp.where(k_iota == k, idx, idxs)
        cur = jnp.where(e_iota == idx, -jnp.inf, cur)
    ex = jnp.exp(vals - jnp.max(vals, axis=-1, keepdims=True))
    te_ref[...] = idxs
    tg_ref[...] = ex / jnp.sum(ex, axis=-1, keepdims=True)


def _outproj(x, merged, wout, gin, bin_, g1, b1, wr, br):
    n = x.shape[0]
    tm = min(256, n)
    const = lambda i: (0, 0)
    row = lambda i: (i, 0)
    return pl.pallas_call(
        _outproj_kernel,
        grid=(n // tm,),
        in_specs=[
            pl.BlockSpec((tm, D_MODEL), row),
            pl.BlockSpec((tm, D_MODEL), row),
            pl.BlockSpec((D_MODEL, D_MODEL), const),
            pl.BlockSpec((1, D_MODEL), const),
            pl.BlockSpec((1, D_MODEL), const),
            pl.BlockSpec((1, D_MODEL), const),
            pl.BlockSpec((1, D_MODEL), const),
            pl.BlockSpec((D_MODEL, 2 * N_EXPERTS), const),
            pl.BlockSpec((1, N_EXPERTS), const),
        ],
        out_specs=[
            pl.BlockSpec((tm, D_MODEL), row),
            pl.BlockSpec((tm * ROW_TILE, LANES), row),
            pl.BlockSpec((tm, TOP_K), row),
            pl.BlockSpec((tm, TOP_K), row),
        ],
        out_shape=[
            jax.ShapeDtypeStruct((n, D_MODEL), F32),
            jax.ShapeDtypeStruct((n * ROW_TILE, LANES), U32),
            jax.ShapeDtypeStruct((n, TOP_K), I32),
            jax.ShapeDtypeStruct((n, TOP_K), F32),
        ],
        compiler_params=_cparams(("parallel",)),
        name="outproj_ln_router",
    )(x, merged, wout, gin, bin_, g1, b1, wr, br)


def _gather_rows(idx_smem, base, src_hbm, dst_vmem, sem, count):
    def body(i, carry):
        src_row = pl.multiple_of(idx_smem[base + i], ROW_TILE)
        dst_row = pl.multiple_of(i * ROW_TILE, ROW_TILE)
        pltpu.make_async_copy(src_hbm.at[pl.ds(src_row, ROW_TILE)], dst_vmem.at[pl.ds(dst_row, ROW_TILE)],
                              sem).start()
        return carry
    lax.fori_loop(0, count, body, 0, unroll=8)


def _wait_rows(src_hbm, dst_vmem, sem, count):
    pltpu.make_async_copy(src_hbm.at[pl.ds(0, count * ROW_TILE)], dst_vmem, sem).wait()


def _moe_kernel(be_ref, nu_ref, sh_ref, idx_hbm, x_hbm, wg_ref, bg_ref, wu_ref, bu_ref, wd_ref, bd_ref, o_ref,
                idx_smem, idx_sem, gbuf, gsem, xb_scr, h_scr, *, tmb, rc, nf, nn, tf, tn):
    b = pl.program_id(0)
    t = pl.program_id(1)
    n_used = nu_ref[0]
    active = b < n_used
    slot = b % 2
    nxt = 1 - slot
    per = tmb // (nf + nn)
    win = 2 * tmb

    def idx_copy(s, sl):
        return pltpu.make_async_copy(idx_hbm.at[pl.ds(pl.multiple_of(s * win, win), win)],
                                     idx_smem.at[pl.ds(pl.multiple_of(sl * win, win), win)],
                                     idx_sem.at[sl])

    def gather_next_slice():
        base = nxt * win + sh_ref[b + 1] + t * per
        for i in range(per):
            src_row = pl.multiple_of(idx_smem[base + i], ROW_TILE)
            dst_row = pl.multiple_of((t * per + i) * ROW_TILE, ROW_TILE)
            pltpu.make_async_copy(x_hbm.at[pl.ds(src_row, ROW_TILE)], gbuf.at[nxt, pl.ds(dst_row, ROW_TILE)],
                                  gsem.at[nxt]).start()

    @pl.when((b <= n_used) & (t == 0))
    def _():
        @pl.when(b == 0)
        def _():
            c = idx_copy(0, 0)
            c.start()
            c.wait()
            _gather_rows(idx_smem, sh_ref[0], x_hbm, gbuf.at[0], gsem.at[0], tmb)
            idx_copy(1, 1).start()

        _wait_rows(x_hbm, gbuf.at[slot], gsem.at[slot], tmb)

    @pl.when(active & (t == 0))
    def _():
        idx_copy(b + 1, nxt).wait()

        @pl.when(b + 2 <= n_used)
        def _():
            idx_copy(b + 2, slot).start()

        def body(c, carry):
            r0 = pl.multiple_of(c * rc, rc)
            r = pl.ds(r0, rc)
            for s in range(ROW_TILE):
                c_lo, c_hi = _packed_chunks(s, D_MODEL)
                lo, hi = _unpack_rows(_load_row_tile_words(gbuf.at[slot], r0, rc, s))
                xb_scr[r, c_lo * LANES:(c_lo + 1) * LANES] = lo.astype(BF16)
                xb_scr[r, c_hi * LANES:(c_hi + 1) * LANES] = hi.astype(BF16)
            return carry
        lax.fori_loop(0, tmb // rc, body, 0)

    @pl.when(active & (t < nf))
    def _():
        gather_next_slice()
        x = xb_scr[...]
        g = jnp.dot(x, wg_ref[0], preferred_element_type=F32) + bg_ref[0]
        u = jnp.dot(x, wu_ref[0], preferred_element_type=F32) + bu_ref[0]
        g = jnp.minimum(g, SWIGLU_LIMIT)
        u = jnp.clip(u, -SWIGLU_LIMIT, SWIGLU_LIMIT)
        h = (g * _sigmoid(SWIGLU_ALPHA * g) * (u + 1.0)).astype(BF16)
        for f in range(nf):
            @pl.when(t == f)
            def _():
                h_scr[:, f * tf:(f + 1) * tf] = h

    @pl.when(active & (t >= nf))
    def _():
        gather_next_slice()
        y = jnp.dot(h_scr[...], wd_ref[0], preferred_element_type=F32) + bd_ref[0]
        sub_per_tile = tn // (2 * LANES)
        s0 = (t - nf) * sub_per_tile
        for c in range(tmb // rc):
            yc = y[c * rc:(c + 1) * rc]
            for q in range(sub_per_tile):
                lo = yc[:, q * LANES:(q + 1) * LANES]
                hi = yc[:, tn // 2 + q * LANES:tn // 2 + (q + 1) * LANES]
                o_ref[pl.ds(c * rc * ROW_TILE + s0 + q, rc, stride=ROW_TILE), :] = _pack_words(lo, hi)

    @pl.when(jnp.logical_not(active) & (t == nf + nn - 1))
    def _():
        o_ref[...] = jnp.zeros(o_ref.shape, U32)


def _moe_experts(block_e, n_used, shift, row_win, x_packed, wg, bg, wu, bu, wd, bd, tmb):
    nblk = block_e.shape[0]
    tf = MOE_TILE
    tn = MOE_TILE
    nf = D_FF // tf
    nn = D_MODEL // tn
    rc = min(256, tmb)

    def fsel(b, t, nu):
        return jnp.where(b < nu[0], jnp.minimum(t, nf - 1), nf - 1)

    def nsel(b, t, nu):
        return jnp.where(b < nu[0], jnp.maximum(t - nf, 0), nn - 1)

    grid_spec = pltpu.PrefetchScalarGridSpec(
        num_scalar_prefetch=3,
        grid=(nblk, nf + nn),
        in_specs=[
            pl.BlockSpec(memory_space=pl.ANY),
            pl.BlockSpec(memory_space=pl.ANY),
            pl.BlockSpec((1, D_MODEL, tf), lambda b, t, be, nu, sh: (be[b], 0, fsel(b, t, nu))),
            pl.BlockSpec((1, 1, tf), lambda b, t, be, nu, sh: (be[b], 0, fsel(b, t, nu))),
            pl.BlockSpec((1, D_MODEL, tf), lambda b, t, be, nu, sh: (be[b], 0, fsel(b, t, nu))),
            pl.BlockSpec((1, 1, tf), lambda b, t, be, nu, sh: (be[b], 0, fsel(b, t, nu))),
            pl.BlockSpec((1, D_FF, tn), lambda b, t, be, nu, sh: (be[b], 0, nsel(b, t, nu))),
            pl.BlockSpec((1, 1, tn), lambda b, t, be, nu, sh: (be[b], 0, nsel(b, t, nu))),
        ],
        out_specs=pl.BlockSpec((tmb * ROW_TILE, LANES), lambda b, t, be, nu, sh: (b, 0)),
        scratch_shapes=[
            pltpu.SMEM((4 * tmb,), I32),
            pltpu.SemaphoreType.DMA((2,)),
            pltpu.VMEM((2, tmb * ROW_TILE, LANES), U32),
            pltpu.SemaphoreType.DMA((2,)),
            pltpu.VMEM((tmb, D_MODEL), BF16),
            pltpu.VMEM((tmb, D_FF), BF16),
        ],
    )
    return pl.pallas_call(
        functools.partial(_moe_kernel, tmb=tmb, rc=rc, nf=nf, nn=nn, tf=tf, tn=tn),
        grid_spec=grid_spec,
        out_shape=jax.ShapeDtypeStruct((nblk * tmb * ROW_TILE, LANES), U32),
        compiler_params=_cparams(("arbitrary", "arbitrary"), disable_bounds_checks=True),
        name="moe_experts",
    )(block_e, n_used, shift, row_win, x_packed, wg, bg, wu, bu, wd, bd)


def _combine_kernel(idx_hbm, y_hbm, x1_ref, gate_ref, g_ref, b_ref, o_ref,
                    idx_smem, idx_sem, gbuf0, gbuf1, gsem, *, tc):
    i = pl.program_id(0)
    last = pl.num_programs(0) - 1
    count = TOP_K * tc
    gbufs = (gbuf0, gbuf1)

    def idx_copy(s, sl):
        return pltpu.make_async_copy(idx_hbm.at[pl.ds(pl.multiple_of(s * count, count), count)],
                                     idx_smem.at[pl.ds(pl.multiple_of(sl * count, count), count)],
                                     idx_sem.at[sl])

    @pl.when(i == 0)
    def _():
        c = idx_copy(0, 0)
        c.start()
        c.wait()
        _gather_rows(idx_smem, 0, y_hbm, gbuf0, gsem.at[0], count)
        idx_copy(jnp.minimum(1, last), 1).start()

    for parity in range(2):
        @pl.when(i % 2 == parity)
        def _():
            _combine_step(parity, i, last, count, tc, idx_copy, idx_smem, y_hbm, gbufs, gsem,
                          x1_ref, gate_ref, g_ref, b_ref, o_ref)


def _combine_step(slot, i, last, count, tc, idx_copy, idx_smem, y_hbm, gbufs, gsem,
                  x1_ref, gate_ref, g_ref, b_ref, o_ref):
    nslot = 1 - slot
    _wait_rows(y_hbm, gbufs[slot], gsem.at[slot], count)
    idx_copy(jnp.minimum(i + 1, last), nslot).wait()
    for r in range(count):
        src_row = pl.multiple_of(idx_smem[nslot * count + r], ROW_TILE)
        pltpu.make_async_copy(y_hbm.at[pl.ds(src_row, ROW_TILE)],
                              gbufs[nslot].at[pl.ds(r * ROW_TILE, ROW_TILE)], gsem.at[nslot]).start()
    idx_copy(jnp.minimum(i + 2, last), slot).start()

    gates = gate_ref[...]
    gk = [gates[:, k:k + 1] for k in range(TOP_K)]
    ssum = jnp.zeros((tc, 1), F32)
    ssq = jnp.zeros((tc, 1), F32)
    for s in range(ROW_TILE):
        chunks = _packed_chunks(s, MOE_TILE)
        z = [DN_ALPHA * x1_ref[:, c * LANES:(c + 1) * LANES] for c in chunks]
        for k in range(TOP_K):
            halves = _unpack_rows(_load_row_tile_words(gbufs[slot], k * tc, tc, s))
            z = [zc + gk[k] * h for zc, h in zip(z, halves)]
        for c, zc in zip(chunks, z):
            o_ref[:, c * LANES:(c + 1) * LANES] = zc
            ssum = ssum + jnp.sum(zc, axis=-1, keepdims=True)
            ssq = ssq + jnp.sum(zc * zc, axis=-1, keepdims=True)
    mu = ssum / D_MODEL
    inv = lax.rsqrt(ssq / D_MODEL - mu * mu + LN_EPS)
    for c in range(D_MODEL // LANES):
        cols = slice(c * LANES, (c + 1) * LANES)
        o_ref[:, cols] = (o_ref[:, cols] - mu) * inv * g_ref[:, cols] + b_ref[:, cols]

    @pl.when(i == last)
    def _():
        _wait_rows(y_hbm, gbufs[nslot], gsem.at[nslot], count)
        idx_copy(last, slot).wait()


def _combine(dest_tiles, yb, x1, gates, g2, b2, tc):
    n = x1.shape[0]
    const = lambda i: (0, 0)
    row = lambda i: (i, 0)
    return pl.pallas_call(
        functools.partial(_combine_kernel, tc=tc),
        grid=(n // tc,),
        in_specs=[
            pl.BlockSpec(memory_space=pl.ANY),
            pl.BlockSpec(memory_space=pl.ANY),
            pl.BlockSpec((tc, D_MODEL), row),
            pl.BlockSpec((tc, TOP_K), row),
            pl.BlockSpec((1, D_MODEL), const),
            pl.BlockSpec((1, D_MODEL), const),
        ],
        out_specs=pl.BlockSpec((tc, D_MODEL), row),
        out_shape=jax.ShapeDtypeStruct((n, D_MODEL), F32),
        scratch_shapes=[
            pltpu.SMEM((2 * TOP_K * tc,), I32),
            pltpu.SemaphoreType.DMA((2,)),
            pltpu.VMEM((TOP_K * tc * ROW_TILE, LANES), U32),
            pltpu.VMEM((TOP_K * tc * ROW_TILE, LANES), U32),
            pltpu.SemaphoreType.DMA((2,)),
        ],
        compiler_params=_cparams(("arbitrary",), disable_bounds_checks=True),
        name="moe_combine_ln",
    )(dest_tiles, yb, x1, gates, g2, b2)


def _route_meta(top_e, tmb):
    nt = top_e.shape[0]
    nblk = (nt * TOP_K) // tmb + N_EXPERTS
    e_ids = jnp.arange(N_EXPERTS, dtype=I32)
    sel = top_e[:, :, None] == e_ids[None, None, :]
    onehot = sel.sum(axis=1).astype(I32)
    incl = jnp.cumsum(onehot, axis=0)
    counts = incl[-1]
    padded = (counts + tmb - 1) // tmb * tmb
    pend = jnp.cumsum(padded)
    pstart = pend - padded
    start = jnp.cumsum(counts) - counts
    dest = jnp.where(sel, (pstart[None, :] + incl - onehot)[:, None, :], 0).sum(axis=-1).astype(I32)
    n_used = (pend[-1] // tmb).astype(I32)
    blk = jnp.arange(nblk, dtype=I32)
    be = jnp.clip(jnp.searchsorted(pend, blk * tmb, side='right'), 0, N_EXPERTS - 1).astype(I32)
    be = jnp.where(blk < n_used, be, be[jnp.maximum(n_used - 1, 0)])
    order = jnp.argsort(top_e.reshape(-1), stable=True).astype(I32)
    lines = (nt * TOP_K) // LANES + 2 * tmb // LANES + 1
    tok_sorted = jnp.zeros((lines * LANES,), I32).at[:nt * TOP_K].set((order // TOP_K) * ROW_TILE)
    win = jnp.clip(blk * tmb - pstart[be] + start[be], 0, nt * TOP_K)
    line_idx = (win // LANES)[:, None] + jnp.arange(2 * tmb // LANES, dtype=I32)[None, :]
    row_win = jnp.take(tok_sorted.reshape(lines, LANES), line_idx, axis=0).reshape(-1)
    shift = (win % LANES).astype(I32)
    return be, n_used.reshape(1), shift, row_win, dest * ROW_TILE


def kernel(x_prompt, x_sample, ln_in_g, ln_in_b, w_in, lam_q1, lam_k1, lam_q2, lam_k2, subln_g, w_attn_proj, ssm_lam_re, ssm_lam_im, ssm_log_dt, ssm_b_re, ssm_b_im, ssm_c_re, ssm_c_im, ssm_d, w_glu, b_glu, w_ssm_proj, w_out, ln1_g, ln1_b, w_router, b_router, w_gate, b_gate, w_up, b_up, w_down, b_down, ln2_g, ln2_b):
    l = 0
    row = lambda v: v.reshape(1, -1).astype(F32)
    lambda_init = 0.8 - 0.6 * math.exp(-0.3 * l)
    lam = (jnp.exp(jnp.sum(lam_q1[l].astype(F32) * lam_k1[l].astype(F32)))
           - jnp.exp(jnp.sum(lam_q2[l].astype(F32) * lam_k2[l].astype(F32))) + lambda_init).reshape(1, 1)
    w_in_b = w_in[l].astype(BF16)
    wglu_b = w_glu[l].astype(BF16)
    wap_b = w_attn_proj[l].astype(BF16)
    wsp_b = w_ssm_proj[l].astype(BF16)
    wout_b = w_out[l].astype(BF16)
    wg_b = w_gate[l].astype(BF16)
    wu_b = w_up[l].astype(BF16)
    wd_b = w_down[l].astype(BF16)
    wr = w_router[l].astype(F32)
    wr_hi = wr.astype(BF16)
    wr_split = jnp.concatenate([wr_hi, (wr - wr_hi.astype(F32)).astype(BF16)], axis=1)
    s5_ops = _s5_operators(ssm_lam_re[l], ssm_lam_im[l], ssm_log_dt[l], ssm_b_re[l], ssm_b_im[l],
                           ssm_c_re[l], ssm_c_im[l])
    d_tiled = jnp.tile(ssm_d[l].astype(F32).reshape(SSM_GROUPS, 1, SSM_GROUP), (1, 1, S5_CHUNK))

    def pre_moe(x3):
        batch, seq, _ = x3.shape
        x = x3.reshape(batch * seq, D_MODEL)
        proj = _inproj(x, row(ln_in_g), row(ln_in_b), w_in_b)
        cos_t, sin_t = _rope_tables(seq)
        attn = _attention(proj, lam, cos_t, sin_t, row(subln_g[l]), batch, seq, lambda_init)
        yact = _s5_branch(proj, s5_ops, d_tiled, batch, seq)
        merged = _merge(yact, attn, proj, wglu_b, row(b_glu[l]), wap_b, wsp_b)
        return _outproj(x, merged, wout_b, row(ln_in_g), row(ln_in_b), row(ln1_g[l]), row(ln1_b[l]),
                        wr_split, row(b_router[l]))

    parts = [pre_moe(x_prompt), pre_moe(x_sample)]
    xp_all = jnp.concatenate([p[1] for p in parts], axis=0)
    te_all = jnp.concatenate([p[2] for p in parts], axis=0)
    nt = xp_all.shape[0]
    tmb = 1024 if nt >= 8192 else 128
    be, n_used, shift, row_win, dest = _route_meta(te_all, tmb)
    yb = _moe_experts(be, n_used, shift, row_win, xp_all, wg_b, b_gate[l].astype(F32)[:, None, :],
                      wu_b, b_up[l].astype(F32)[:, None, :], wd_b, b_down[l].astype(F32)[:, None, :], tmb)

    outs = []
    off = 0
    for x3, p in zip((x_prompt, x_sample), parts):
        n = p[0].shape[0]
        tc = min(256, n)
        d = dest[off:off + n].reshape(n // tc, tc, TOP_K)
        d = jnp.transpose(d, (0, 2, 1)).reshape(-1)
        out = _combine(d, yb, p[0], p[3], row(ln2_g[l]), row(ln2_b[l]), tc)
        outs.append(out.reshape(x3.shape))
        off += n
    return tuple(outs)
```

```python
import functools
import math

import jax
import jax.numpy as jnp
from jax import lax
from jax.experimental import pallas as pl
from jax.experimental.pallas import tpu as pltpu

F32 = jnp.float32
BF16 = jnp.bfloat16
U32 = jnp.uint32
I32 = jnp.int32

D_MODEL = 2048
DEPTH = 1
N_HEADS = 8
HEAD_DIM = 64
HEAD_W = 2 * HEAD_DIM
ATTN_W = N_HEADS * HEAD_W
ROT_DIM = HEAD_DIM // 4
ROPE_THETA = 500000.0
SSM_GROUP = 16
SSM_GROUPS = 64
D_SSM = SSM_GROUP * SSM_GROUPS
SSM_STATE = 64
S5_CHUNK = 16
S5_W = S5_CHUNK * SSM_GROUP
N_IN = 3 * ATTN_W + D_SSM + 2 * D_MODEL
N_EXPERTS = 32
TOP_K = 4
D_FF = D_MODEL
SWIGLU_LIMIT = 7.0
SWIGLU_ALPHA = 1.702
DN_ALPHA = (2.0 * DEPTH) ** 0.25
LN_EPS = 1e-5
HALF = D_MODEL // 2
LANES = 128
ROW_TILE = 8
MOE_TILE = 512
MOE_TILE_DOWN = 1024

VMEM_LIMIT = 56 * 1024 * 1024


def _cparams(sem, **kw):
    return pltpu.CompilerParams(dimension_semantics=sem, vmem_limit_bytes=VMEM_LIMIT, **kw)


def _ln(x, g, b):
    mu = jnp.mean(x, axis=-1, keepdims=True)
    xc = x - mu
    var = jnp.mean(xc * xc, axis=-1, keepdims=True)
    return xc * lax.rsqrt(var + LN_EPS) * g + b


def _sigmoid(x):
    return 1.0 / (1.0 + jnp.exp(-x))


def _gelu_tanh(x):
    return 0.5 * x * (1.0 + jnp.tanh(math.sqrt(2.0 / math.pi) * (x + 0.044715 * (x * x * x))))


def _pack_words(lo, hi):
    lo = lax.bitcast_convert_type(lo.astype(BF16).astype(F32), U32)
    hi = lax.bitcast_convert_type(hi.astype(BF16).astype(F32), U32)
    return (hi & jnp.uint32(0xFFFF0000)) | (lo >> 16)


def _pack_rows(x):
    return _pack_words(x[:, :HALF], x[:, HALF:])


def _packed_chunks(s, tile):
    sub_per_tile = tile // (2 * LANES)
    n, q = divmod(s, sub_per_tile)
    lo = n * (tile // LANES) + q
    return lo, lo + sub_per_tile


def _unpack_rows(w):
    lo = lax.bitcast_convert_type(w << 16, F32)
    hi = lax.bitcast_convert_type(w & jnp.uint32(0xFFFF0000), F32)
    return lo, hi


def _store_row_tiles(ref, r0, nrows, packed):
    for s in range(ROW_TILE):
        ref[pl.ds(r0 * ROW_TILE + s, nrows, stride=ROW_TILE), :] = packed[:, s * LANES:(s + 1) * LANES]


def _load_row_tile_words(ref, r0, nrows, s):
    return ref[pl.ds(r0 * ROW_TILE + s, nrows, stride=ROW_TILE), :]


def _inproj_kernel(x_ref, g_ref, b_ref, w_ref, o_ref, xn_ref, *, tm, rc):
    @pl.when(pl.program_id(1) == 0)
    def _():
        def body(c, carry):
            r = pl.ds(pl.multiple_of(c * rc, rc), rc)
            xn_ref[r, :] = _ln(x_ref[r, :], g_ref[...], b_ref[...]).astype(BF16)
            return carry
        lax.fori_loop(0, tm // rc, body, 0)

    o_ref[...] = jnp.dot(xn_ref[...], w_ref[...], preferred_element_type=F32).astype(BF16)


def _inproj(x, g, b, w_bf16):
    n = x.shape[0]
    tm = min(1024, n)
    tn = 1024
    rc = min(256, tm)
    return pl.pallas_call(
        functools.partial(_inproj_kernel, tm=tm, rc=rc),
        grid=(n // tm, N_IN // tn),
        in_specs=[
            pl.BlockSpec((tm, D_MODEL), lambda i, j: (i, 0)),
            pl.BlockSpec((1, D_MODEL), lambda i, j: (0, 0)),
            pl.BlockSpec((1, D_MODEL), lambda i, j: (0, 0)),
            pl.BlockSpec((D_MODEL, tn), lambda i, j: (0, j)),
        ],
        out_specs=pl.BlockSpec((tm, tn), lambda i, j: (i, j)),
        out_shape=jax.ShapeDtypeStruct((n, N_IN), BF16),
        scratch_shapes=[pltpu.VMEM((tm, D_MODEL), BF16)],
        compiler_params=_cparams(("parallel", "arbitrary")),
        name="ln_inproj",
    )(x, g, b, w_bf16)


def _rope(x, c, s):
    lane = lax.broadcasted_iota(I32, (1, HEAD_W), 1)
    first_half = (lane % HEAD_DIM) < (ROT_DIM // 2)
    xs = jnp.where(first_half, pltpu.roll(x, HEAD_W - ROT_DIM // 2, 1), pltpu.roll(x, ROT_DIM // 2, 1))
    return x * c + xs * s


ONES_ROWS = 16


def _attn_kernel(lam_ref, q_ref, k_ref, v_ref, cq_ref, sq_ref, ck_ref, sk_ref, g_ref, o_ref,
                 k_scr, vt_scr, q_scr, *, seq, tq, tk, out_scale):
    nkv = seq // tk

    @pl.when(pl.program_id(2) == 0)
    def _():
        def body(c, carry):
            r = pl.ds(pl.multiple_of(c * tk, tk), tk)
            k_scr[r, :] = _rope(k_ref[r, :].astype(F32), ck_ref[r, :], sk_ref[r, :]).astype(BF16)
            vt_scr[c, :HEAD_W, :] = v_ref[r, :].astype(F32).T.astype(BF16)
            vt_scr[c, HEAD_W:, :] = jnp.ones((ONES_ROWS, tk), BF16)
            return carry
        lax.fori_loop(0, nkv, body, 0)

    q = _rope(q_ref[...].astype(F32), cq_ref[...], sq_ref[...]) * (HEAD_DIM ** -0.5 * math.log2(math.e))
    lane = lax.broadcasted_iota(I32, (1, HEAD_W), 1)
    comp1 = lane < HEAD_DIM
    q_scr[pl.ds(0, tq), :] = jnp.where(comp1, q, 0.0).astype(BF16)
    q_scr[pl.ds(tq, tq), :] = jnp.where(comp1, 0.0, q).astype(BF16)

    def scores(j):
        return lax.dot_general(k_scr[j * tk:(j + 1) * tk, :], q_scr[...], (((1,), (1,)), ((), ())),
                               preferred_element_type=F32)

    m = jnp.full((1, 2 * tq), -jnp.inf, F32)
    acc = jnp.zeros((HEAD_W + ONES_ROWS, 2 * tq), F32)
    s_cur = scores(0)
    for j in range(nkv):
        s_next = scores(j + 1) if j + 1 < nkv else None
        m_new = jnp.maximum(m, jnp.max(s_cur, axis=0, keepdims=True))
        p = jnp.exp2(s_cur - m_new).astype(BF16)
        acc = jnp.exp2(m - m_new) * acc + jnp.dot(vt_scr[j], p, preferred_element_type=F32)
        m = m_new
        s_cur = s_next

    o = acc[:HEAD_W] / acc[HEAD_W:HEAD_W + 1]
    o = o[:, :tq] - lam_ref[0, 0] * o[:, tq:]
    o = o * lax.rsqrt(jnp.mean(o * o, axis=0, keepdims=True) + LN_EPS) * g_ref[...] * out_scale
    o_ref[...] = o.T.astype(BF16)


def _attention(proj, lam, cos_t, sin_t, subln_g, batch, seq, lambda_init):
    tq = min(512, seq)
    tk = min(1024, seq)
    nq = seq // tq
    kern = functools.partial(_attn_kernel, seq=seq, tq=tq, tk=tk, out_scale=1.0 - lambda_init)
    return pl.pallas_call(
        kern,
        grid=(batch, N_HEADS, nq),
        in_specs=[
            pl.BlockSpec(memory_space=pltpu.SMEM),
            pl.BlockSpec((tq, HEAD_W), lambda b, h, i: (b * nq + i, h)),
            pl.BlockSpec((seq, HEAD_W), lambda b, h, i: (b, N_HEADS + h)),
            pl.BlockSpec((seq, HEAD_W), lambda b, h, i: (b, 2 * N_HEADS + h)),
            pl.BlockSpec((tq, HEAD_W), lambda b, h, i: (i, 0)),
            pl.BlockSpec((tq, HEAD_W), lambda b, h, i: (i, 0)),
            pl.BlockSpec((seq, HEAD_W), lambda b, h, i: (0, 0)),
            pl.BlockSpec((seq, HEAD_W), lambda b, h, i: (0, 0)),
            pl.BlockSpec((HEAD_W, 1), lambda b, h, i: (0, 0)),
        ],
        out_specs=pl.BlockSpec((tq, HEAD_W), lambda b, h, i: (b * nq + i, h)),
        out_shape=jax.ShapeDtypeStruct((batch * seq, ATTN_W), BF16),
        scratch_shapes=[
            pltpu.VMEM((seq, HEAD_W), BF16),
            pltpu.VMEM((seq // tk, HEAD_W + ONES_ROWS, tk), BF16),
            pltpu.VMEM((2 * tq, HEAD_W), BF16),
        ],
        compiler_params=_cparams(("parallel", "parallel", "arbitrary")),
        name="diff_attention",
    )(lam, proj, proj, proj, cos_t, sin_t, cos_t, sin_t, subln_g.reshape(HEAD_W, 1))


def _rope_tables(seq):
    half = ROT_DIM // 2
    inv = ROPE_THETA ** (-jnp.arange(0, ROT_DIM, 2, dtype=F32) / ROT_DIM)
    ang = jnp.arange(seq, dtype=F32)[:, None] * inv[None, :]
    lane = jnp.arange(HEAD_W) % HEAD_DIM
    cos_l = jnp.cos(ang)[:, lane % half]
    sin_l = jnp.sin(ang)[:, lane % half]
    cos_t = jnp.where(lane[None, :] < ROT_DIM, cos_l, 1.0)
    sin_t = jnp.where(lane[None, :] < half, -sin_l, jnp.where(lane[None, :] < ROT_DIM, sin_l, 0.0))
    return cos_t.astype(F32), sin_t.astype(F32)


def _s5_operators(lam_re, lam_im, log_dt, b_re, b_im, c_re, c_im):
    L, G, P, CG = S5_CHUNK, SSM_GROUPS, SSM_STATE, SSM_GROUP
    hp = lax.Precision.HIGHEST
    lr = jnp.minimum(lam_re.astype(F32), -1e-4)
    li = lam_im.astype(F32)
    dt = jnp.exp(log_dt.astype(F32))[..., None]
    nn = jnp.arange(L + 1, dtype=F32)[:, None, None, None]
    mag = jnp.exp(nn * (lr * dt)[None])
    pw_r = mag * jnp.cos(nn * (li * dt)[None])
    pw_i = mag * jnp.sin(nn * (li * dt)[None])
    ab_r, ab_i = pw_r[1], pw_i[1]
    nr = ab_r - 1.0
    den = lr * lr + li * li
    cr_ = ((nr * lr + ab_i * li) / den)[..., None]
    ci_ = ((ab_i * lr - nr * li) / den)[..., None]
    br = b_re.astype(F32)
    bi = b_im.astype(F32)
    bb_r = cr_ * br - ci_ * bi
    bb_i = cr_ * bi + ci_ * br
    cr = c_re.astype(F32)
    ci = c_im.astype(F32)

    m_r = pw_r[..., None] * bb_r[None] - pw_i[..., None] * bb_i[None]
    m_i = pw_r[..., None] * bb_i[None] + pw_i[..., None] * bb_r[None]
    kk = (jnp.einsum('dgop,ndgpi->ndgoi', cr, m_r, precision=hp)
          - jnp.einsum('dgop,ndgpi->ndgoi', ci, m_i, precision=hp))
    ii = jnp.arange(L)[:, None]
    jj = jnp.arange(L)[None, :]
    kf = jnp.where((jj >= ii)[:, :, None, None, None], kk[:, 0][jnp.clip(jj - ii, 0, L)], 0.0)
    kb = jnp.where((ii >= jj)[:, :, None, None, None], kk[:, 1][jnp.clip(ii - jj, 0, L)], 0.0)
    t_op = jnp.transpose(kf + kb, (2, 0, 4, 1, 3)).reshape(G, S5_W, S5_W)

    def state_in(mr, mi):
        e_re = jnp.transpose(mr, (1, 0, 3, 2)).reshape(G, S5_W, P)
        e_im = jnp.transpose(mi, (1, 0, 3, 2)).reshape(G, S5_W, P)
        return jnp.concatenate([e_re, e_im, e_im, e_re], axis=-1)
    bm = jnp.concatenate([state_in(m_r[L - 1::-1, 0][:L], m_i[L - 1::-1, 0][:L]),
                          state_in(m_r[:L, 1], m_i[:L, 1])], axis=-1)

    def state_out(d, pows):
        pr = pw_r[pows, d]
        pi = pw_i[pows, d]
        g_r = cr[d][None] * pr[:, :, None, :] - ci[d][None] * pi[:, :, None, :]
        g_i = cr[d][None] * pi[:, :, None, :] + ci[d][None] * pr[:, :, None, :]
        rows_re = jnp.transpose(g_r, (1, 3, 0, 2)).reshape(G, P, S5_W)
        rows_im = jnp.transpose(-g_i, (1, 3, 0, 2)).reshape(G, P, S5_W)
        return jnp.concatenate([rows_re, rows_im], axis=1)
    cm = jnp.concatenate([state_out(0, jnp.arange(1, L + 1)), state_out(1, jnp.arange(L, 0, -1))], axis=1)

    def step_mult(d):
        ar = jnp.concatenate([pw_r[L, d]] * 4, axis=-1)
        ai = jnp.concatenate([-pw_i[L, d], pw_i[L, d], pw_i[L, d], -pw_i[L, d]], axis=-1)
        return ar, ai
    arf, aif = step_mult(0)
    arb, aib = step_mult(1)
    a_r = jnp.concatenate([arf, arb], axis=-1)[:, None, :]
    a_i = jnp.concatenate([aif, aib], axis=-1)[:, None, :]
    return t_op.astype(BF16), bm.astype(BF16), cm.astype(BF16), a_r, a_i


S5_LANE_GROUPS = LANES // SSM_GROUP
S5_CAT = S5_CHUNK * LANES


def _s5_kernel(u_ref, t_ref, bm_ref, cm_ref, ar_ref, ai_ref, d_ref, o_ref,
               perm_scr, stage_scr, uall_scr, e_scr, s_scr, *, nc, nb, seg):
    rows = nb * nc
    w2 = S5_W
    p2 = 2 * SSM_STATE

    @pl.when((pl.program_id(0) == 0) & (pl.program_id(1) == 0))
    def _():
        def body(t, carry):
            gc = lax.broadcasted_iota(I32, (LANES, S5_CAT), 0)
            tgt = (gc // SSM_GROUP) * S5_W + t * SSM_GROUP + gc % SSM_GROUP
            col = lax.broadcasted_iota(I32, (LANES, S5_CAT), 1)
            r0 = pl.multiple_of(t * LANES, LANES)
            perm_scr[pl.ds(r0, LANES), :] = jnp.where(col == tgt, 1.0, 0.0).astype(BF16)
            return carry
        lax.fori_loop(0, S5_CHUNK, body, 0)

    def relayout_in(k, carry):
        tok0 = pl.multiple_of(k * (seg * S5_CHUNK), seg * S5_CHUNK)
        stage_scr[...] = u_ref[pl.ds(tok0, seg * S5_CHUNK), :].astype(F32)
        z = jnp.concatenate([stage_scr[pl.ds(t, seg, stride=S5_CHUNK), :] for t in range(S5_CHUNK)], axis=1)
        r0 = pl.multiple_of(k * seg, seg)
        uall_scr[pl.ds(r0, seg), :] = jnp.dot(z.astype(BF16), perm_scr[...],
                                              preferred_element_type=F32).astype(BF16)
        return carry
    lax.fori_loop(0, rows // seg, relayout_in, 0)

    for gp in range(S5_LANE_GROUPS // 2):
        pair = (2 * gp, 2 * gp + 1)
        for q, g in enumerate(pair):
            e = jnp.dot(uall_scr[:, g * w2:(g + 1) * w2], bm_ref[g], preferred_element_type=F32)
            for j in range(4):
                e_scr[4 * q + j] = e[:, j * p2:(j + 1) * p2]
        mult = [[(ar_ref[g, :, j * p2:(j + 1) * p2], ai_ref[g, :, j * p2:(j + 1) * p2]) for j in range(4)]
                for g in pair]

        def body(c, carry):
            rf = pl.ds(c, nb, stride=nc)
            rb = pl.ds(nc - 1 - c, nb, stride=nc)
            out = []
            for q in range(2):
                xf, xfs, zb, zbs = carry[4 * q:4 * q + 4]
                m = mult[q]
                s_scr[2 * q, rf, :] = xf
                s_scr[2 * q + 1, rb, :] = zb
                out += [m[0][0] * xf + m[0][1] * xfs + e_scr[4 * q, rf, :],
                        m[1][0] * xfs + m[1][1] * xf + e_scr[4 * q + 1, rf, :],
                        m[2][0] * zb + m[2][1] * zbs + e_scr[4 * q + 2, rb, :],
                        m[3][0] * zbs + m[3][1] * zb + e_scr[4 * q + 3, rb, :]]
            return tuple(out)
        z0 = jnp.zeros((nb, p2), F32)
        lax.fori_loop(0, nc, body, (z0,) * 8, unroll=2)

        for q, g in enumerate(pair):
            ug = uall_scr[:, g * w2:(g + 1) * w2]
            st = jnp.concatenate([s_scr[2 * q], s_scr[2 * q + 1]], axis=1).astype(BF16)
            y = (jnp.dot(ug, t_ref[g], preferred_element_type=F32)
                 + jnp.dot(st, cm_ref[g], preferred_element_type=F32)
                 + d_ref[g] * ug.astype(F32))
            uall_scr[:, g * w2:(g + 1) * w2] = _gelu_tanh(y).astype(BF16)

    def relayout_out(k, carry):
        r0 = pl.multiple_of(k * seg, seg)
        yt = lax.dot_general(uall_scr[pl.ds(r0, seg), :], perm_scr[...], (((1,), (1,)), ((), ())),
                             preferred_element_type=F32)
        for t in range(S5_CHUNK):
            stage_scr[pl.ds(t, seg, stride=S5_CHUNK), :] = yt[:, t * LANES:(t + 1) * LANES]
        tok0 = pl.multiple_of(k * (seg * S5_CHUNK), seg * S5_CHUNK)
        o_ref[pl.ds(tok0, seg * S5_CHUNK), :] = stage_scr[...].astype(BF16)
        return carry
    lax.fori_loop(0, rows // seg, relayout_out, 0)


def _s5_branch(proj, ops, d_tiled, batch, seq):
    t_op, bm, cm, a_r, a_i = ops
    n = batch * seq
    nc = seq // S5_CHUNK
    rows_blk = min(16384, n)
    nb = rows_blk // seq
    seg = min(256, nb * nc)
    ucol = (3 * ATTN_W) // LANES
    lg = S5_LANE_GROUPS
    blk = lambda gb, sb: (gb, 0, 0)
    return pl.pallas_call(
        functools.partial(_s5_kernel, nc=nc, nb=nb, seg=seg),
        grid=(D_SSM // LANES, n // rows_blk),
        in_specs=[
            pl.BlockSpec((rows_blk, LANES), lambda gb, sb: (sb, ucol + gb)),
            pl.BlockSpec((lg, S5_W, S5_W), blk),
            pl.BlockSpec((lg, S5_W, 2 * S5_W), blk),
            pl.BlockSpec((lg, S5_W, S5_W), blk),
            pl.BlockSpec((lg, 1, 2 * S5_W), blk),
            pl.BlockSpec((lg, 1, 2 * S5_W), blk),
            pl.BlockSpec((lg, 1, S5_W), blk),
        ],
        out_specs=pl.BlockSpec((rows_blk, LANES), lambda gb, sb: (sb, gb)),
        out_shape=jax.ShapeDtypeStruct((n, D_SSM), BF16),
        scratch_shapes=[
            pltpu.VMEM((S5_CAT, S5_CAT), BF16),
            pltpu.VMEM((seg * S5_CHUNK, LANES), F32),
            pltpu.VMEM((nb * nc, S5_CAT), BF16),
            pltpu.VMEM((8, nb * nc, LANES), F32),
            pltpu.VMEM((4, nb * nc, LANES), F32),
        ],
        compiler_params=_cparams(("arbitrary", "arbitrary")),
        name="s5_chunked",
    )(proj, t_op, bm, cm, a_r, a_i, d_tiled)


def _merge_kernel(y_ref, a_ref, gs_ref, ga_ref, wglu_ref, bglu_ref, wap_ref, wsp_ref, o_ref):
    y = y_ref[...]
    z = jnp.dot(y, wglu_ref[...], preferred_element_type=F32) + bglu_ref[...]
    act = (y.astype(F32) * _sigmoid(z)).astype(BF16)
    pa = jnp.dot(a_ref[...], wap_ref[...], preferred_element_type=F32)
    ps = jnp.dot(act, wsp_ref[...], preferred_element_type=F32)
    o = _sigmoid(ga_ref[...].astype(F32)) * pa + _sigmoid(gs_ref[...].astype(F32)) * ps
    o_ref[...] = o.astype(BF16)


def _merge(yact, attn, proj, wglu, bglu, wap, wsp):
    n = yact.shape[0]
    tm = min(256, n)
    gcol = (3 * ATTN_W + D_SSM) // D_MODEL
    const = lambda i: (0, 0)
    return pl.pallas_call(
        _merge_kernel,
        grid=(n // tm,),
        in_specs=[
            pl.BlockSpec((tm, D_SSM), lambda i: (i, 0)),
            pl.BlockSpec((tm, ATTN_W), lambda i: (i, 0)),
            pl.BlockSpec((tm, D_MODEL), lambda i: (i, gcol)),
            pl.BlockSpec((tm, D_MODEL), lambda i: (i, gcol + 1)),
            pl.BlockSpec((D_SSM, D_SSM), const),
            pl.BlockSpec((1, D_SSM), const),
            pl.BlockSpec((ATTN_W, D_MODEL), const),
            pl.BlockSpec((D_SSM, D_MODEL), const),
        ],
        out_specs=pl.BlockSpec((tm, D_MODEL), lambda i: (i, 0)),
        out_shape=jax.ShapeDtypeStruct((n, D_MODEL), BF16),
        compiler_params=_cparams(("parallel",)),
        name="glu_merge",
    )(yact, attn, proj, proj, wglu, bglu, wap, wsp)


def _outproj_kernel(x_ref, m_ref, wout_ref, gin_ref, bin_ref, g1_ref, b1_ref, wr_ref, br_ref,
                    x1_ref, xp_ref, te_ref, tg_ref):
    xn = _ln(x_ref[...], gin_ref[...], bin_ref[...])
    z = DN_ALPHA * xn + jnp.dot(m_ref[...], wout_ref[...], preferred_element_type=F32)
    x1 = _ln(z, g1_ref[...], b1_ref[...])
    x1_ref[...] = x1
    _store_row_tiles(xp_ref, 0, x1.shape[0], _pack_rows(x1))

    x_hi = x1.astype(BF16)
    x_lo = (x1 - x_hi.astype(F32)).astype(BF16)
    r = (jnp.dot(x_hi, wr_ref[...], preferred_element_type=F32)
         + jnp.dot(x_lo, wr_ref[...], preferred_element_type=F32))
    logits = r[:, :N_EXPERTS] + r[:, N_EXPERTS:] + br_ref[...]
    e_iota = lax.broadcasted_iota(I32, logits.shape, 1)
    k_iota = lax.broadcasted_iota(I32, (logits.shape[0], TOP_K), 1)
    vals = jnp.zeros((logits.shape[0], TOP_K), F32)
    idxs = jnp.zeros((logits.shape[0], TOP_K), I32)
    cur = logits
    for k in range(TOP_K):
        m = jnp.max(cur, axis=-1, keepdims=True)
        idx = jnp.min(jnp.where(cur == m, e_iota, N_EXPERTS), axis=-1, keepdims=True)
        vals = jnp.where(k_iota == k, m, vals)
        idxs = jnp.where(k_iota == k, idx, idxs)
        cur = jnp.where(e_iota == idx, -jnp.inf, cur)
    ex = jnp.exp(vals - jnp.max(vals, axis=-1, keepdims=True))
    te_ref[...] = idxs
    tg_ref[...] = ex / jnp.sum(ex, axis=-1, keepdims=True)


def _outproj(x, merged, wout, gin, bin_, g1, b1, wr, br):
    n = x.shape[0]
    tm = min(256, n)
    const = lambda i: (0, 0)
    row = lambda i: (i, 0)
    return pl.pallas_call(
        _outproj_kernel,
        grid=(n // tm,),
        in_specs=[
            pl.BlockSpec((tm, D_MODEL), row),
            pl.BlockSpec((tm, D_MODEL), row),
            pl.BlockSpec((D_MODEL, D_MODEL), const),
            pl.BlockSpec((1, D_MODEL), const),
            pl.BlockSpec((1, D_MODEL), const),
            pl.BlockSpec((1, D_MODEL), const),
            pl.BlockSpec((1, D_MODEL), const),
            pl.BlockSpec((D_MODEL, 2 * N_EXPERTS), const),
            pl.BlockSpec((1, N_EXPERTS), const),
        ],
        out_specs=[
            pl.BlockSpec((tm, D_MODEL), row),
            pl.BlockSpec((tm * ROW_TILE, LANES), row),
            pl.BlockSpec((tm, TOP_K), row),
            pl.BlockSpec((tm, TOP_K), row),
        ],
        out_shape=[
            jax.ShapeDtypeStruct((n, D_MODEL), F32),
            jax.ShapeDtypeStruct((n * ROW_TILE, LANES), U32),
            jax.ShapeDtypeStruct((n, TOP_K), I32),
            jax.ShapeDtypeStruct((n, TOP_K), F32),
        ],
        compiler_params=_cparams(("parallel",)),
        name="outproj_ln_router",
    )(x, merged, wout, gin, bin_, g1, b1, wr, br)


def _gather_rows(idx_smem, base, src_hbm, dst_vmem, sem, count):
    def body(i, carry):
        src_row = pl.multiple_of(idx_smem[base + i], ROW_TILE)
        dst_row = pl.multiple_of(i * ROW_TILE, ROW_TILE)
        pltpu.make_async_copy(src_hbm.at[pl.ds(src_row, ROW_TILE)], dst_vmem.at[pl.ds(dst_row, ROW_TILE)],
                              sem).start()
        return carry
    lax.fori_loop(0, count, body, 0, unroll=8)


def _wait_rows(src_hbm, dst_vmem, sem, count):
    pltpu.make_async_copy(src_hbm.at[pl.ds(0, count * ROW_TILE)], dst_vmem, sem).wait()


def _moe_kernel(be_ref, nu_ref, sh_ref, idx_hbm, x_hbm, wg_ref, bg_ref, wu_ref, bu_ref, wd_ref, bd_ref, o_ref,
                idx_smem, idx_sem, gbuf, gsem, xb_scr, h_scr, *, tmb, rc, nf, nn, tf, tn):
    b = pl.program_id(0)
    t = pl.program_id(1)
    n_used = nu_ref[0]
    active = b < n_used
    slot = b % 2
    nxt = 1 - slot
    per = tmb // nf
    win = 2 * tmb

    def idx_copy(s, sl):
        return pltpu.make_async_copy(idx_hbm.at[pl.ds(pl.multiple_of(s * win, win), win)],
                                     idx_smem.at[pl.ds(pl.multiple_of(sl * win, win), win)],
                                     idx_sem.at[sl])

    def gather_next_slice():
        base = nxt * win + sh_ref[b + 1] + t * per
        for i in range(per):
            src_row = pl.multiple_of(idx_smem[base + i], ROW_TILE)
            dst_row = pl.multiple_of((t * per + i) * ROW_TILE, ROW_TILE)
            pltpu.make_async_copy(x_hbm.at[pl.ds(src_row, ROW_TILE)], gbuf.at[nxt, pl.ds(dst_row, ROW_TILE)],
                                  gsem.at[nxt]).start()

    @pl.when((b <= n_used) & (t == 0))
    def _():
        @pl.when(b == 0)
        def _():
            c = idx_copy(0, 0)
            c.start()
            c.wait()
            _gather_rows(idx_smem, sh_ref[0], x_hbm, gbuf.at[0], gsem.at[0], tmb)
            idx_copy(1, 1).start()

        _wait_rows(x_hbm, gbuf.at[slot], gsem.at[slot], tmb)

    @pl.when(active & (t == 0))
    def _():
        idx_copy(b + 1, nxt).wait()

        @pl.when(b + 2 <= n_used)
        def _():
            idx_copy(b + 2, slot).start()

        def body(c, carry):
            r0 = pl.multiple_of(c * rc, rc)
            r = pl.ds(r0, rc)
            for s in range(ROW_TILE):
                c_lo, c_hi = _packed_chunks(s, D_MODEL)
                lo, hi = _unpack_rows(_load_row_tile_words(gbuf.at[slot], r0, rc, s))
                xb_scr[r, c_lo * LANES:(c_lo + 1) * LANES] = lo.astype(BF16)
                xb_scr[r, c_hi * LANES:(c_hi + 1) * LANES] = hi.astype(BF16)
            return carry
        lax.fori_loop(0, tmb // rc, body, 0)

    @pl.when(active & (t < nf))
    def _():
        gather_next_slice()
        x = xb_scr[...]
        g = jnp.dot(x, wg_ref[0], preferred_element_type=F32) + bg_ref[0]
        u = jnp.dot(x, wu_ref[0], preferred_element_type=F32) + bu_ref[0]
        g = jnp.minimum(g, SWIGLU_LIMIT)
        u = jnp.clip(u, -SWIGLU_LIMIT, SWIGLU_LIMIT)
        h = (g * _sigmoid(SWIGLU_ALPHA * g) * (u + 1.0)).astype(BF16)
        for f in range(nf):
            @pl.when(t == f)
            def _():
                h_scr[:, f * tf:(f + 1) * tf] = h

    @pl.when(active & (t >= nf))
    def _():
        y = jnp.dot(h_scr[...], wd_ref[0], preferred_element_type=F32) + bd_ref[0]
        sub_per_tile = tn // (2 * LANES)
        s0 = (t - nf) * sub_per_tile
        for c in range(tmb // rc):
            yc = y[c * rc:(c + 1) * rc]
            for q in range(sub_per_tile):
                lo = yc[:, q * LANES:(q + 1) * LANES]
                hi = yc[:, tn // 2 + q * LANES:tn // 2 + (q + 1) * LANES]
                o_ref[pl.ds(c * rc * ROW_TILE + s0 + q, rc, stride=ROW_TILE), :] = _pack_words(lo, hi)

    @pl.when(jnp.logical_not(active) & (t == nf + nn - 1))
    def _():
        o_ref[...] = jnp.zeros(o_ref.shape, U32)


def _moe_experts(block_e, n_used, shift, row_win, x_packed, wg, bg, wu, bu, wd, bd, tmb):
    nblk = block_e.shape[0]
    tf = MOE_TILE
    tn = MOE_TILE_DOWN
    nf = D_FF // tf
    nn = D_MODEL // tn
    rc = min(256, tmb)

    def fsel(b, t, nu):
        return jnp.where(b < nu[0], jnp.minimum(t, nf - 1), nf - 1)

    def nsel(b, t, nu):
        return jnp.where(b < nu[0], jnp.maximum(t - nf, 0), nn - 1)

    grid_spec = pltpu.PrefetchScalarGridSpec(
        num_scalar_prefetch=3,
        grid=(nblk, nf + nn),
        in_specs=[
            pl.BlockSpec(memory_space=pl.ANY),
            pl.BlockSpec(memory_space=pl.ANY),
            pl.BlockSpec((1, D_MODEL, tf), lambda b, t, be, nu, sh: (be[b], 0, fsel(b, t, nu))),
            pl.BlockSpec((1, 1, tf), lambda b, t, be, nu, sh: (be[b], 0, fsel(b, t, nu))),
            pl.BlockSpec((1, D_MODEL, tf), lambda b, t, be, nu, sh: (be[b], 0, fsel(b, t, nu))),
            pl.BlockSpec((1, 1, tf), lambda b, t, be, nu, sh: (be[b], 0, fsel(b, t, nu))),
            pl.BlockSpec((1, D_FF, tn), lambda b, t, be, nu, sh: (be[b], 0, nsel(b, t, nu))),
            pl.BlockSpec((1, 1, tn), lambda b, t, be, nu, sh: (be[b], 0, nsel(b, t, nu))),
        ],
        out_specs=pl.BlockSpec((tmb * ROW_TILE, LANES), lambda b, t, be, nu, sh: (b, 0)),
        scratch_shapes=[
            pltpu.SMEM((4 * tmb,), I32),
            pltpu.SemaphoreType.DMA((2,)),
            pltpu.VMEM((2, tmb * ROW_TILE, LANES), U32),
            pltpu.SemaphoreType.DMA((2,)),
            pltpu.VMEM((tmb, D_MODEL), BF16),
            pltpu.VMEM((tmb, D_FF), BF16),
        ],
    )
    return pl.pallas_call(
        functools.partial(_moe_kernel, tmb=tmb, rc=rc, nf=nf, nn=nn, tf=tf, tn=tn),
        grid_spec=grid_spec,
        out_shape=jax.ShapeDtypeStruct((nblk * tmb * ROW_TILE, LANES), U32),
        compiler_params=_cparams(("arbitrary", "arbitrary"), disable_bounds_checks=True),
        name="moe_experts",
    )(block_e, n_used, shift, row_win, x_packed, wg, bg, wu, bu, wd, bd)


def _combine_kernel(idx_hbm, y_hbm, x1_ref, gate_ref, g_ref, b_ref, o_ref,
                    idx_smem, idx_sem, gbuf0, gbuf1, gsem, *, tc):
    i = pl.program_id(0)
    last = pl.num_programs(0) - 1
    count = TOP_K * tc
    gbufs = (gbuf0, gbuf1)

    def idx_copy(s, sl):
        return pltpu.make_async_copy(idx_hbm.at[pl.ds(pl.multiple_of(s * count, count), count)],
                                     idx_smem.at[pl.ds(pl.multiple_of(sl * count, count), count)],
                                     idx_sem.at[sl])

    @pl.when(i == 0)
    def _():
        c = idx_copy(0, 0)
        c.start()
        c.wait()
        _gather_rows(idx_smem, 0, y_hbm, gbuf0, gsem.at[0], count)
        idx_copy(jnp.minimum(1, last), 1).start()

    for parity in range(2):
        @pl.when(i % 2 == parity)
        def _():
            _combine_step(parity, i, last, count, tc, idx_copy, idx_smem, y_hbm, gbufs, gsem,
                          x1_ref, gate_ref, g_ref, b_ref, o_ref)


def _combine_step(slot, i, last, count, tc, idx_copy, idx_smem, y_hbm, gbufs, gsem,
                  x1_ref, gate_ref, g_ref, b_ref, o_ref):
    nslot = 1 - slot
    _wait_rows(y_hbm, gbufs[slot], gsem.at[slot], count)
    idx_copy(jnp.minimum(i + 1, last), nslot).wait()
    for r in range(count):
        src_row = pl.multiple_of(idx_smem[nslot * count + r], ROW_TILE)
        pltpu.make_async_copy(y_hbm.at[pl.ds(src_row, ROW_TILE)],
                              gbufs[nslot].at[pl.ds(r * ROW_TILE, ROW_TILE)], gsem.at[nslot]).start()
    idx_copy(jnp.minimum(i + 2, last), slot).start()

    gates = gate_ref[...]
    gk = [gates[:, k:k + 1] for k in range(TOP_K)]
    ssum = jnp.zeros((tc, 1), F32)
    ssq = jnp.zeros((tc, 1), F32)
    for s in range(ROW_TILE):
        chunks = _packed_chunks(s, MOE_TILE_DOWN)
        z = [DN_ALPHA * x1_ref[:, c * LANES:(c + 1) * LANES] for c in chunks]
        for k in range(TOP_K):
            halves = _unpack_rows(_load_row_tile_words(gbufs[slot], k * tc, tc, s))
            z = [zc + gk[k] * h for zc, h in zip(z, halves)]
        for c, zc in zip(chunks, z):
            o_ref[:, c * LANES:(c + 1) * LANES] = zc
            ssum = ssum + jnp.sum(zc, axis=-1, keepdims=True)
            ssq = ssq + jnp.sum(zc * zc, axis=-1, keepdims=True)
    mu = ssum / D_MODEL
    inv = lax.rsqrt(ssq / D_MODEL - mu * mu + LN_EPS)
    for c in range(D_MODEL // LANES):
        cols = slice(c * LANES, (c + 1) * LANES)
        o_ref[:, cols] = (o_ref[:, cols] - mu) * inv * g_ref[:, cols] + b_ref[:, cols]

    @pl.when(i == last)
    def _():
        _wait_rows(y_hbm, gbufs[nslot], gsem.at[nslot], count)
        idx_copy(last, slot).wait()


def _combine(dest_tiles, yb, x1, gates, g2, b2, tc):
    n = x1.shape[0]
    const = lambda i: (0, 0)
    row = lambda i: (i, 0)
    return pl.pallas_call(
        functools.partial(_combine_kernel, tc=tc),
        grid=(n // tc,),
        in_specs=[
            pl.BlockSpec(memory_space=pl.ANY),
            pl.BlockSpec(memory_space=pl.ANY),
            pl.BlockSpec((tc, D_MODEL), row),
            pl.BlockSpec((tc, TOP_K), row),
            pl.BlockSpec((1, D_MODEL), const),
            pl.BlockSpec((1, D_MODEL), const),
        ],
        out_specs=pl.BlockSpec((tc, D_MODEL), row),
        out_shape=jax.ShapeDtypeStruct((n, D_MODEL), F32),
        scratch_shapes=[
            pltpu.SMEM((2 * TOP_K * tc,), I32),
            pltpu.SemaphoreType.DMA((2,)),
            pltpu.VMEM((TOP_K * tc * ROW_TILE, LANES), U32),
            pltpu.VMEM((TOP_K * tc * ROW_TILE, LANES), U32),
            pltpu.SemaphoreType.DMA((2,)),
        ],
        compiler_params=_cparams(("arbitrary",), disable_bounds_checks=True),
        name="moe_combine_ln",
    )(dest_tiles, yb, x1, gates, g2, b2)


def _route_meta(top_e, tmb):
    nt = top_e.shape[0]
    nblk = (nt * TOP_K) // tmb + N_EXPERTS
    e_ids = jnp.arange(N_EXPERTS, dtype=I32)
    sel = top_e[:, :, None] == e_ids[None, None, :]
    onehot = sel.sum(axis=1).astype(I32)
    incl = jnp.cumsum(onehot, axis=0)
    counts = incl[-1]
    padded = (counts + tmb - 1) // tmb * tmb
    pend = jnp.cumsum(padded)
    pstart = pend - padded
    start = jnp.cumsum(counts) - counts
    dest = jnp.where(sel, (pstart[None, :] + incl - onehot)[:, None, :], 0).sum(axis=-1).astype(I32)
    n_used = (pend[-1] // tmb).astype(I32)
    blk = jnp.arange(nblk, dtype=I32)
    be = jnp.clip(jnp.searchsorted(pend, blk * tmb, side='right'), 0, N_EXPERTS - 1).astype(I32)
    be = jnp.where(blk < n_used, be, be[jnp.maximum(n_used - 1, 0)])
    order = jnp.argsort(top_e.reshape(-1), stable=True).astype(I32)
    lines = (nt * TOP_K) // LANES + 2 * tmb // LANES + 1
    tok_sorted = jnp.zeros((lines * LANES,), I32).at[:nt * TOP_K].set((order // TOP_K) * ROW_TILE)
    win = jnp.clip(blk * tmb - pstart[be] + start[be], 0, nt * TOP_K)
    line_idx = (win // LANES)[:, None] + jnp.arange(2 * tmb // LANES, dtype=I32)[None, :]
    row_win = jnp.take(tok_sorted.reshape(lines, LANES), line_idx, axis=0).reshape(-1)
    shift = (win % LANES).astype(I32)
    return be, n_used.reshape(1), shift, row_win, dest * ROW_TILE


def kernel(x_prompt, x_sample, ln_in_g, ln_in_b, w_in, lam_q1, lam_k1, lam_q2, lam_k2, subln_g, w_attn_proj, ssm_lam_re, ssm_lam_im, ssm_log_dt, ssm_b_re, ssm_b_im, ssm_c_re, ssm_c_im, ssm_d, w_glu, b_glu, w_ssm_proj, w_out, ln1_g, ln1_b, w_router, b_router, w_gate, b_gate, w_up, b_up, w_down, b_down, ln2_g, ln2_b):
    l = 0
    row = lambda v: v.reshape(1, -1).astype(F32)
    lambda_init = 0.8 - 0.6 * math.exp(-0.3 * l)
    lam = (jnp.exp(jnp.sum(lam_q1[l].astype(F32) * lam_k1[l].astype(F32)))
           - jnp.exp(jnp.sum(lam_q2[l].astype(F32) * lam_k2[l].astype(F32))) + lambda_init).reshape(1, 1)
    w_in_b = w_in[l].astype(BF16)
    wglu_b = w_glu[l].astype(BF16)
    wap_b = w_attn_proj[l].astype(BF16)
    wsp_b = w_ssm_proj[l].astype(BF16)
    wout_b = w_out[l].astype(BF16)
    wg_b = w_gate[l].astype(BF16)
    wu_b = w_up[l].astype(BF16)
    wd_b = w_down[l].astype(BF16)
    wr = w_router[l].astype(F32)
    wr_hi = wr.astype(BF16)
    wr_split = jnp.concatenate([wr_hi, (wr - wr_hi.astype(F32)).astype(BF16)], axis=1)
    s5_ops = _s5_operators(ssm_lam_re[l], ssm_lam_im[l], ssm_log_dt[l], ssm_b_re[l], ssm_b_im[l],
                           ssm_c_re[l], ssm_c_im[l])
    d_tiled = jnp.tile(ssm_d[l].astype(F32).reshape(SSM_GROUPS, 1, SSM_GROUP), (1, 1, S5_CHUNK))

    def pre_moe(x3):
        batch, seq, _ = x3.shape
        x = x3.reshape(batch * seq, D_MODEL)
        proj = _inproj(x, row(ln_in_g), row(ln_in_b), w_in_b)
        cos_t, sin_t = _rope_tables(seq)
        attn = _attention(proj, lam, cos_t, sin_t, row(subln_g[l]), batch, seq, lambda_init)
        yact = _s5_branch(proj, s5_ops, d_tiled, batch, seq)
        merged = _merge(yact, attn, proj, wglu_b, row(b_glu[l]), wap_b, wsp_b)
        return _outproj(x, merged, wout_b, row(ln_in_g), row(ln_in_b), row(ln1_g[l]), row(ln1_b[l]),
                        wr_split, row(b_router[l]))

    parts = [pre_moe(x_prompt), pre_moe(x_sample)]
    xp_all = jnp.concatenate([p[1] for p in parts], axis=0)
    te_all = jnp.concatenate([p[2] for p in parts], axis=0)
    nt = xp_all.shape[0]
    tmb = 1024 if nt >= 8192 else 128
    be, n_used, shift, row_win, dest = _route_meta(te_all, tmb)
    yb = _moe_experts(be, n_used, shift, row_win, xp_all, wg_b, b_gate[l].astype(F32)[:, None, :],
                      wu_b, b_up[l].astype(F32)[:, None, :], wd_b, b_down[l].astype(F32)[:, None, :], tmb)

    outs = []
    off = 0
    for x3, p in zip((x_prompt, x_sample), parts):
        n = p[0].shape[0]
        tc = min(256, n)
        d = dest[off:off + n].reshape(n // tc, tc, TOP_K)
        d = jnp.transpose(d, (0, 2, 1)).reshape(-1)
        out = _combine(d, yb, p[0], p[3], row(ln2_g[l]), row(ln2_b[l]), tc)
        outs.append(out.reshape(x3.shape))
        off += n
    return tuple(outs)
```

```python
import functools
import math

import jax
import jax.numpy as jnp
from jax import lax
from jax.experimental import pallas as pl
from jax.experimental.pallas import tpu as pltpu

F32 = jnp.float32
BF16 = jnp.bfloat16
U32 = jnp.uint32
I32 = jnp.int32

D_MODEL = 2048
DEPTH = 1
N_HEADS = 8
HEAD_DIM = 64
HEAD_W = 2 * HEAD_DIM
ATTN_W = N_HEADS * HEAD_W
ROT_DIM = HEAD_DIM // 4
ROPE_THETA = 500000.0
SSM_GROUP = 16
SSM_GROUPS = 64
D_SSM = SSM_GROUP * SSM_GROUPS
SSM_STATE = 64
S5_CHUNK = 16
S5_W = S5_CHUNK * SSM_GROUP
N_IN = 3 * ATTN_W + D_SSM + 2 * D_MODEL
N_EXPERTS = 32
TOP_K = 4
D_FF = D_MODEL
SWIGLU_LIMIT = 7.0
SWIGLU_ALPHA = 1.702
DN_ALPHA = (2.0 * DEPTH) ** 0.25
LN_EPS = 1e-5
HALF = D_MODEL // 2
LANES = 128
ROW_TILE = 8
MOE_TILE = 512
MOE_TILE_DOWN = 1024

VMEM_LIMIT = 56 * 1024 * 1024


def _cparams(sem, **kw):
    return pltpu.CompilerParams(dimension_semantics=sem, vmem_limit_bytes=VMEM_LIMIT, **kw)


def _ln(x, g, b):
    mu = jnp.mean(x, axis=-1, keepdims=True)
    xc = x - mu
    var = jnp.mean(xc * xc, axis=-1, keepdims=True)
    return xc * lax.rsqrt(var + LN_EPS) * g + b


def _sigmoid(x):
    return 1.0 / (1.0 + jnp.exp(-x))


def _gelu_tanh(x):
    return 0.5 * x * (1.0 + jnp.tanh(math.sqrt(2.0 / math.pi) * (x + 0.044715 * (x * x * x))))


def _pack_words(lo, hi):
    lo = lax.bitcast_convert_type(lo.astype(BF16).astype(F32), U32)
    hi = lax.bitcast_convert_type(hi.astype(BF16).astype(F32), U32)
    return (hi & jnp.uint32(0xFFFF0000)) | (lo >> 16)


def _pack_rows(x):
    return _pack_words(x[:, :HALF], x[:, HALF:])


def _packed_chunks(s, tile):
    sub_per_tile = tile // (2 * LANES)
    n, q = divmod(s, sub_per_tile)
    lo = n * (tile // LANES) + q
    return lo, lo + sub_per_tile


def _unpack_rows(w):
    lo = lax.bitcast_convert_type(w << 16, F32)
    hi = lax.bitcast_convert_type(w & jnp.uint32(0xFFFF0000), F32)
    return lo, hi


def _store_row_tiles(ref, r0, nrows, packed):
    for s in range(ROW_TILE):
        ref[pl.ds(r0 * ROW_TILE + s, nrows, stride=ROW_TILE), :] = packed[:, s * LANES:(s + 1) * LANES]


def _load_row_tile_words(ref, r0, nrows, s):
    return ref[pl.ds(r0 * ROW_TILE + s, nrows, stride=ROW_TILE), :]


def _inproj_kernel(x_ref, g_ref, b_ref, w_ref, o_ref, xn_ref, *, tm, rc):
    @pl.when(pl.program_id(1) == 0)
    def _():
        def body(c, carry):
            r = pl.ds(pl.multiple_of(c * rc, rc), rc)
            xn_ref[r, :] = _ln(x_ref[r, :], g_ref[...], b_ref[...]).astype(BF16)
            return carry
        lax.fori_loop(0, tm // rc, body, 0)

    o_ref[...] = jnp.dot(xn_ref[...], w_ref[...], preferred_element_type=F32).astype(BF16)


def _inproj(x, g, b, w_bf16):
    n = x.shape[0]
    tm = min(1024, n)
    tn = 1024
    rc = min(256, tm)
    return pl.pallas_call(
        functools.partial(_inproj_kernel, tm=tm, rc=rc),
        grid=(n // tm, N_IN // tn),
        in_specs=[
            pl.BlockSpec((tm, D_MODEL), lambda i, j: (i, 0)),
            pl.BlockSpec((1, D_MODEL), lambda i, j: (0, 0)),
            pl.BlockSpec((1, D_MODEL), lambda i, j: (0, 0)),
            pl.BlockSpec((D_MODEL, tn), lambda i, j: (0, j)),
        ],
        out_specs=pl.BlockSpec((tm, tn), lambda i, j: (i, j)),
        out_shape=jax.ShapeDtypeStruct((n, N_IN), BF16),
        scratch_shapes=[pltpu.VMEM((tm, D_MODEL), BF16)],
        compiler_params=_cparams(("parallel", "arbitrary")),
        name="ln_inproj",
    )(x, g, b, w_bf16)


def _rope(x, c, s):
    lane = lax.broadcasted_iota(I32, (1, HEAD_W), 1)
    first_half = (lane % HEAD_DIM) < (ROT_DIM // 2)
    xs = jnp.where(first_half, pltpu.roll(x, HEAD_W - ROT_DIM // 2, 1), pltpu.roll(x, ROT_DIM // 2, 1))
    return x * c + xs * s


ONES_ROWS = 16


def _attn_kernel(lam_ref, q_ref, k_ref, v_ref, cq_ref, sq_ref, ck_ref, sk_ref, g_ref, o_ref,
                 k_scr, vt_scr, q_scr, *, seq, tq, tk, out_scale):
    nkv = seq // tk

    @pl.when(pl.program_id(2) == 0)
    def _():
        def body(c, carry):
            r = pl.ds(pl.multiple_of(c * tk, tk), tk)
            k_scr[r, :] = _rope(k_ref[r, :].astype(F32), ck_ref[r, :], sk_ref[r, :]).astype(BF16)
            vt_scr[c, :HEAD_W, :] = v_ref[r, :].astype(F32).T.astype(BF16)
            vt_scr[c, HEAD_W:, :] = jnp.ones((ONES_ROWS, tk), BF16)
            return carry
        lax.fori_loop(0, nkv, body, 0)

    q = _rope(q_ref[...].astype(F32), cq_ref[...], sq_ref[...]) * (HEAD_DIM ** -0.5 * math.log2(math.e))
    lane = lax.broadcasted_iota(I32, (1, HEAD_W), 1)
    comp1 = lane < HEAD_DIM
    q_scr[pl.ds(0, tq), :] = jnp.where(comp1, q, 0.0).astype(BF16)
    q_scr[pl.ds(tq, tq), :] = jnp.where(comp1, 0.0, q).astype(BF16)

    def scores(j):
        return lax.dot_general(k_scr[j * tk:(j + 1) * tk, :], q_scr[...], (((1,), (1,)), ((), ())),
                               preferred_element_type=F32)

    m = jnp.full((1, 2 * tq), -jnp.inf, F32)
    acc = jnp.zeros((HEAD_W + ONES_ROWS, 2 * tq), F32)
    s_cur = scores(0)
    for j in range(nkv):
        s_next = scores(j + 1) if j + 1 < nkv else None
        m_new = jnp.maximum(m, jnp.max(s_cur, axis=0, keepdims=True))
        p = jnp.exp2(s_cur - m_new).astype(BF16)
        acc = jnp.exp2(m - m_new) * acc + jnp.dot(vt_scr[j], p, preferred_element_type=F32)
        m = m_new
        s_cur = s_next

    o = acc[:HEAD_W] / acc[HEAD_W:HEAD_W + 1]
    o = o[:, :tq] - lam_ref[0, 0] * o[:, tq:]
    o = o * lax.rsqrt(jnp.mean(o * o, axis=0, keepdims=True) + LN_EPS) * g_ref[...] * out_scale
    o_ref[...] = o.T.astype(BF16)


def _attention(proj, lam, cos_t, sin_t, subln_g, batch, seq, lambda_init):
    tq = min(512, seq)
    tk = min(1024, seq)
    nq = seq // tq
    kern = functools.partial(_attn_kernel, seq=seq, tq=tq, tk=tk, out_scale=1.0 - lambda_init)
    return pl.pallas_call(
        kern,
        grid=(batch, N_HEADS, nq),
        in_specs=[
            pl.BlockSpec(memory_space=pltpu.SMEM),
            pl.BlockSpec((tq, HEAD_W), lambda b, h, i: (b * nq + i, h)),
            pl.BlockSpec((seq, HEAD_W), lambda b, h, i: (b, N_HEADS + h)),
            pl.BlockSpec((seq, HEAD_W), lambda b, h, i: (b, 2 * N_HEADS + h)),
            pl.BlockSpec((tq, HEAD_W), lambda b, h, i: (i, 0)),
            pl.BlockSpec((tq, HEAD_W), lambda b, h, i: (i, 0)),
            pl.BlockSpec((seq, HEAD_W), lambda b, h, i: (0, 0)),
            pl.BlockSpec((seq, HEAD_W), lambda b, h, i: (0, 0)),
            pl.BlockSpec((HEAD_W, 1), lambda b, h, i: (0, 0)),
        ],
        out_specs=pl.BlockSpec((tq, HEAD_W), lambda b, h, i: (b * nq + i, h)),
        out_shape=jax.ShapeDtypeStruct((batch * seq, ATTN_W), BF16),
        scratch_shapes=[
            pltpu.VMEM((seq, HEAD_W), BF16),
            pltpu.VMEM((seq // tk, HEAD_W + ONES_ROWS, tk), BF16),
            pltpu.VMEM((2 * tq, HEAD_W), BF16),
        ],
        compiler_params=_cparams(("parallel", "parallel", "arbitrary")),
        name="diff_attention",
    )(lam, proj, proj, proj, cos_t, sin_t, cos_t, sin_t, subln_g.reshape(HEAD_W, 1))


def _rope_tables(seq):
    half = ROT_DIM // 2
    inv = ROPE_THETA ** (-jnp.arange(0, ROT_DIM, 2, dtype=F32) / ROT_DIM)
    ang = jnp.arange(seq, dtype=F32)[:, None] * inv[None, :]
    lane = jnp.arange(HEAD_W) % HEAD_DIM
    cos_l = jnp.cos(ang)[:, lane % half]
    sin_l = jnp.sin(ang)[:, lane % half]
    cos_t = jnp.where(lane[None, :] < ROT_DIM, cos_l, 1.0)
    sin_t = jnp.where(lane[None, :] < half, -sin_l, jnp.where(lane[None, :] < ROT_DIM, sin_l, 0.0))
    return cos_t.astype(F32), sin_t.astype(F32)


def _s5_operators(lam_re, lam_im, log_dt, b_re, b_im, c_re, c_im):
    L, G, P, CG = S5_CHUNK, SSM_GROUPS, SSM_STATE, SSM_GROUP
    hp = lax.Precision.HIGHEST
    lr = jnp.minimum(lam_re.astype(F32), -1e-4)
    li = lam_im.astype(F32)
    dt = jnp.exp(log_dt.astype(F32))[..., None]
    nn = jnp.arange(L + 1, dtype=F32)[:, None, None, None]
    mag = jnp.exp(nn * (lr * dt)[None])
    pw_r = mag * jnp.cos(nn * (li * dt)[None])
    pw_i = mag * jnp.sin(nn * (li * dt)[None])
    ab_r, ab_i = pw_r[1], pw_i[1]
    nr = ab_r - 1.0
    den = lr * lr + li * li
    cr_ = ((nr * lr + ab_i * li) / den)[..., None]
    ci_ = ((ab_i * lr - nr * li) / den)[..., None]
    br = b_re.astype(F32)
    bi = b_im.astype(F32)
    bb_r = cr_ * br - ci_ * bi
    bb_i = cr_ * bi + ci_ * br
    cr = c_re.astype(F32)
    ci = c_im.astype(F32)

    m_r = pw_r[..., None] * bb_r[None] - pw_i[..., None] * bb_i[None]
    m_i = pw_r[..., None] * bb_i[None] + pw_i[..., None] * bb_r[None]
    kk = (jnp.einsum('dgop,ndgpi->ndgoi', cr, m_r, precision=hp)
          - jnp.einsum('dgop,ndgpi->ndgoi', ci, m_i, precision=hp))
    ii = jnp.arange(L)[:, None]
    jj = jnp.arange(L)[None, :]
    kf = jnp.where((jj >= ii)[:, :, None, None, None], kk[:, 0][jnp.clip(jj - ii, 0, L)], 0.0)
    kb = jnp.where((ii >= jj)[:, :, None, None, None], kk[:, 1][jnp.clip(ii - jj, 0, L)], 0.0)
    t_op = jnp.transpose(kf + kb, (2, 0, 4, 1, 3)).reshape(G, S5_W, S5_W)

    def state_in(mr, mi):
        e_re = jnp.transpose(mr, (1, 0, 3, 2)).reshape(G, S5_W, P)
        e_im = jnp.transpose(mi, (1, 0, 3, 2)).reshape(G, S5_W, P)
        return jnp.concatenate([e_re, e_im, e_im, e_re], axis=-1)
    bm = jnp.concatenate([state_in(m_r[L - 1::-1, 0][:L], m_i[L - 1::-1, 0][:L]),
                          state_in(m_r[:L, 1], m_i[:L, 1])], axis=-1)

    def state_out(d, pows):
        pr = pw_r[pows, d]
        pi = pw_i[pows, d]
        g_r = cr[d][None] * pr[:, :, None, :] - ci[d][None] * pi[:, :, None, :]
        g_i = cr[d][None] * pi[:, :, None, :] + ci[d][None] * pr[:, :, None, :]
        rows_re = jnp.transpose(g_r, (1, 3, 0, 2)).reshape(G, P, S5_W)
        rows_im = jnp.transpose(-g_i, (1, 3, 0, 2)).reshape(G, P, S5_W)
        return jnp.concatenate([rows_re, rows_im], axis=1)
    cm = jnp.concatenate([state_out(0, jnp.arange(1, L + 1)), state_out(1, jnp.arange(L, 0, -1))], axis=1)

    def step_mult(d):
        ar = jnp.concatenate([pw_r[L, d]] * 4, axis=-1)
        ai = jnp.concatenate([-pw_i[L, d], pw_i[L, d], pw_i[L, d], -pw_i[L, d]], axis=-1)
        return ar, ai
    arf, aif = step_mult(0)
    arb, aib = step_mult(1)
    a_r = jnp.concatenate([arf, arb], axis=-1)[:, None, :]
    a_i = jnp.concatenate([aif, aib], axis=-1)[:, None, :]
    return t_op.astype(BF16), bm.astype(BF16), cm.astype(BF16), a_r, a_i


S5_LANE_GROUPS = LANES // SSM_GROUP
S5_CAT = S5_CHUNK * LANES


def _s5_kernel(u_ref, t_ref, bm_ref, cm_ref, ar_ref, ai_ref, d_ref, o_ref,
               perm_scr, stage_scr, uall_scr, e_scr, s_scr, *, nc, nb, seg):
    rows = nb * nc
    w2 = S5_W
    p2 = 2 * SSM_STATE

    @pl.when((pl.program_id(0) == 0) & (pl.program_id(1) == 0))
    def _():
        def body(t, carry):
            gc = lax.broadcasted_iota(I32, (LANES, S5_CAT), 0)
            tgt = (gc // SSM_GROUP) * S5_W + t * SSM_GROUP + gc % SSM_GROUP
            col = lax.broadcasted_iota(I32, (LANES, S5_CAT), 1)
            r0 = pl.multiple_of(t * LANES, LANES)
            perm_scr[pl.ds(r0, LANES), :] = jnp.where(col == tgt, 1.0, 0.0).astype(BF16)
            return carry
        lax.fori_loop(0, S5_CHUNK, body, 0)

    def relayout_in(k, carry):
        tok0 = pl.multiple_of(k * (seg * S5_CHUNK), seg * S5_CHUNK)
        stage_scr[...] = u_ref[pl.ds(tok0, seg * S5_CHUNK), :].astype(F32)
        z = jnp.concatenate([stage_scr[pl.ds(t, seg, stride=S5_CHUNK), :] for t in range(S5_CHUNK)], axis=1)
        r0 = pl.multiple_of(k * seg, seg)
        uall_scr[pl.ds(r0, seg), :] = jnp.dot(z.astype(BF16), perm_scr[...],
                                              preferred_element_type=F32).astype(BF16)
        return carry
    lax.fori_loop(0, rows // seg, relayout_in, 0)

    for gp in range(S5_LANE_GROUPS // 2):
        pair = (2 * gp, 2 * gp + 1)
        for q, g in enumerate(pair):
            e = jnp.dot(uall_scr[:, g * w2:(g + 1) * w2], bm_ref[g], preferred_element_type=F32)
            for j in range(4):
                e_scr[4 * q + j] = e[:, j * p2:(j + 1) * p2]
        mult = [[(ar_ref[g, :, j * p2:(j + 1) * p2], ai_ref[g, :, j * p2:(j + 1) * p2]) for j in range(4)]
                for g in pair]

        def body(c, carry):
            rf = pl.ds(c, nb, stride=nc)
            rb = pl.ds(nc - 1 - c, nb, stride=nc)
            out = []
            for q in range(2):
                xf, xfs, zb, zbs = carry[4 * q:4 * q + 4]
                m = mult[q]
                s_scr[2 * q, rf, :] = xf
                s_scr[2 * q + 1, rb, :] = zb
                out += [m[0][0] * xf + m[0][1] * xfs + e_scr[4 * q, rf, :],
                        m[1][0] * xfs + m[1][1] * xf + e_scr[4 * q + 1, rf, :],
                        m[2][0] * zb + m[2][1] * zbs + e_scr[4 * q + 2, rb, :],
                        m[3][0] * zbs + m[3][1] * zb + e_scr[4 * q + 3, rb, :]]
            return tuple(out)
        z0 = jnp.zeros((nb, p2), F32)
        lax.fori_loop(0, nc, body, (z0,) * 8, unroll=2)

        for q, g in enumerate(pair):
            ug = uall_scr[:, g * w2:(g + 1) * w2]
            st = jnp.concatenate([s_scr[2 * q], s_scr[2 * q + 1]], axis=1).astype(BF16)
            y = (jnp.dot(ug, t_ref[g], preferred_element_type=F32)
                 + jnp.dot(st, cm_ref[g], preferred_element_type=F32)
                 + d_ref[g] * ug.astype(F32))
            uall_scr[:, g * w2:(g + 1) * w2] = _gelu_tanh(y).astype(BF16)

    def relayout_out(k, carry):
        r0 = pl.multiple_of(k * seg, seg)
        yt = lax.dot_general(uall_scr[pl.ds(r0, seg), :], perm_scr[...], (((1,), (1,)), ((), ())),
                             preferred_element_type=F32)
        for t in range(S5_CHUNK):
            stage_scr[pl.ds(t, seg, stride=S5_CHUNK), :] = yt[:, t * LANES:(t + 1) * LANES]
        tok0 = pl.multiple_of(k * (seg * S5_CHUNK), seg * S5_CHUNK)
        o_ref[pl.ds(tok0, seg * S5_CHUNK), :] = stage_scr[...].astype(BF16)
        return carry
    lax.fori_loop(0, rows // seg, relayout_out, 0)


def _s5_branch(proj, ops, d_tiled, batch, seq):
    t_op, bm, cm, a_r, a_i = ops
    n = batch * seq
    nc = seq // S5_CHUNK
    rows_blk = min(16384, n)
    nb = rows_blk // seq
    seg = min(256, nb * nc)
    ucol = (3 * ATTN_W) // LANES
    lg = S5_LANE_GROUPS
    blk = lambda gb, sb: (gb, 0, 0)
    return pl.pallas_call(
        functools.partial(_s5_kernel, nc=nc, nb=nb, seg=seg),
        grid=(D_SSM // LANES, n // rows_blk),
        in_specs=[
            pl.BlockSpec((rows_blk, LANES), lambda gb, sb: (sb, ucol + gb)),
            pl.BlockSpec((lg, S5_W, S5_W), blk),
            pl.BlockSpec((lg, S5_W, 2 * S5_W), blk),
            pl.BlockSpec((lg, S5_W, S5_W), blk),
            pl.BlockSpec((lg, 1, 2 * S5_W), blk),
            pl.BlockSpec((lg, 1, 2 * S5_W), blk),
            pl.BlockSpec((lg, 1, S5_W), blk),
        ],
        out_specs=pl.BlockSpec((rows_blk, LANES), lambda gb, sb: (sb, gb)),
        out_shape=jax.ShapeDtypeStruct((n, D_SSM), BF16),
        scratch_shapes=[
            pltpu.VMEM((S5_CAT, S5_CAT), BF16),
            pltpu.VMEM((seg * S5_CHUNK, LANES), F32),
            pltpu.VMEM((nb * nc, S5_CAT), BF16),
            pltpu.VMEM((8, nb * nc, LANES), F32),
            pltpu.VMEM((4, nb * nc, LANES), F32),
        ],
        compiler_params=_cparams(("arbitrary", "arbitrary")),
        name="s5_chunked",
    )(proj, t_op, bm, cm, a_r, a_i, d_tiled)


def _merge_kernel(y_ref, a_ref, gs_ref, ga_ref, wglu_ref, bglu_ref, wap_ref, wsp_ref, o_ref):
    y = y_ref[...]
    z = jnp.dot(y, wglu_ref[...], preferred_element_type=F32) + bglu_ref[...]
    act = (y.astype(F32) * _sigmoid(z)).astype(BF16)
    pa = jnp.dot(a_ref[...], wap_ref[...], preferred_element_type=F32)
    ps = jnp.dot(act, wsp_ref[...], preferred_element_type=F32)
    o = _sigmoid(ga_ref[...].astype(F32)) * pa + _sigmoid(gs_ref[...].astype(F32)) * ps
    o_ref[...] = o.astype(BF16)


def _merge(yact, attn, proj, wglu, bglu, wap, wsp):
    n = yact.shape[0]
    tm = min(256, n)
    gcol = (3 * ATTN_W + D_SSM) // D_MODEL
    const = lambda i: (0, 0)
    return pl.pallas_call(
        _merge_kernel,
        grid=(n // tm,),
        in_specs=[
            pl.BlockSpec((tm, D_SSM), lambda i: (i, 0)),
            pl.BlockSpec((tm, ATTN_W), lambda i: (i, 0)),
            pl.BlockSpec((tm, D_MODEL), lambda i: (i, gcol)),
            pl.BlockSpec((tm, D_MODEL), lambda i: (i, gcol + 1)),
            pl.BlockSpec((D_SSM, D_SSM), const),
            pl.BlockSpec((1, D_SSM), const),
            pl.BlockSpec((ATTN_W, D_MODEL), const),
            pl.BlockSpec((D_SSM, D_MODEL), const),
        ],
        out_specs=pl.BlockSpec((tm, D_MODEL), lambda i: (i, 0)),
        out_shape=jax.ShapeDtypeStruct((n, D_MODEL), BF16),
        compiler_params=_cparams(("parallel",)),
        name="glu_merge",
    )(yact, attn, proj, proj, wglu, bglu, wap, wsp)


def _outproj_kernel(x_ref, m_ref, wout_ref, gin_ref, bin_ref, g1_ref, b1_ref, wr_ref, br_ref,
                    x1_ref, xp_ref, te_ref, tg_ref):
    xn = _ln(x_ref[...], gin_ref[...], bin_ref[...])
    z = DN_ALPHA * xn + jnp.dot(m_ref[...], wout_ref[...], preferred_element_type=F32)
    x1 = _ln(z, g1_ref[...], b1_ref[...])
    x1_ref[...] = x1
    _store_row_tiles(xp_ref, 0, x1.shape[0], _pack_rows(x1))

    x_hi = x1.astype(BF16)
    x_lo = (x1 - x_hi.astype(F32)).astype(BF16)
    r = (jnp.dot(x_hi, wr_ref[...], preferred_element_type=F32)
         + jnp.dot(x_lo, wr_ref[...], preferred_element_type=F32))
    logits = r[:, :N_EXPERTS] + r[:, N_EXPERTS:] + br_ref[...]
    e_iota = lax.broadcasted_iota(I32, logits.shape, 1)
    k_iota = lax.broadcasted_iota(I32, (logits.shape[0], TOP_K), 1)
    vals = jnp.zeros((logits.shape[0], TOP_K), F32)
    idxs = jnp.zeros((logits.shape[0], TOP_K), I32)
    cur = logits
    for k in range(TOP_K):
        m = jnp.max(cur, axis=-1, keepdims=True)
        idx = jnp.min(jnp.where(cur == m, e_iota, N_EXPERTS), axis=-1, keepdims=True)
        vals = jnp.where(k_iota == k, m, vals)
        idxs = jnp.where(k_iota == k, idx, idxs)
        cur = jnp.where(e_iota == idx, -jnp.inf, cur)
    ex = jnp.exp(vals - jnp.max(vals, axis=-1, keepdims=True))
    te_ref[...] = idxs
    tg_ref[...] = ex / jnp.sum(ex, axis=-1, keepdims=True)


def _outproj(x, merged, wout, gin, bin_, g1, b1, wr, br):
    n = x.shape[0]
    tm = min(256, n)
    const = lambda i: (0, 0)
    row = lambda i: (i, 0)
    return pl.pallas_call(
        _outproj_kernel,
        grid=(n // tm,),
        in_specs=[
            pl.BlockSpec((tm, D_MODEL), row),
            pl.BlockSpec((tm, D_MODEL), row),
            pl.BlockSpec((D_MODEL, D_MODEL), const),
            pl.BlockSpec((1, D_MODEL), const),
            pl.BlockSpec((1, D_MODEL), const),
            pl.BlockSpec((1, D_MODEL), const),
            pl.BlockSpec((1, D_MODEL), const),
            pl.BlockSpec((D_MODEL, 2 * N_EXPERTS), const),
            pl.BlockSpec((1, N_EXPERTS), const),
        ],
        out_specs=[
            pl.BlockSpec((tm, D_MODEL), row),
            pl.BlockSpec((tm * ROW_TILE, LANES), row),
            pl.BlockSpec((tm, TOP_K), row),
            pl.BlockSpec((tm, TOP_K), row),
        ],
        out_shape=[
            jax.ShapeDtypeStruct((n, D_MODEL), F32),
            jax.ShapeDtypeStruct((n * ROW_TILE, LANES), U32),
            jax.ShapeDtypeStruct((n, TOP_K), I32),
            jax.ShapeDtypeStruct((n, TOP_K), F32),
        ],
        compiler_params=_cparams(("parallel",)),
        name="outproj_ln_router",
    )(x, merged, wout, gin, bin_, g1, b1, wr, br)


def _gather_rows(idx_smem, base, src_hbm, dst_vmem, sem, count):
    def body(i, carry):
        src_row = pl.multiple_of(idx_smem[base + i], ROW_TILE)
        dst_row = pl.multiple_of(i * ROW_TILE, ROW_TILE)
        pltpu.make_async_copy(src_hbm.at[pl.ds(src_row, ROW_TILE)], dst_vmem.at[pl.ds(dst_row, ROW_TILE)],
                              sem).start()
        return carry
    lax.fori_loop(0, count, body, 0, unroll=8)


def _wait_rows(src_hbm, dst_vmem, sem, count):
    pltpu.make_async_copy(src_hbm.at[pl.ds(0, count * ROW_TILE)], dst_vmem, sem).wait()


def _moe_kernel(be_ref, nu_ref, sh_ref, idx_hbm, x_hbm, wg_ref, bg_ref, wu_ref, bu_ref, wd_ref, bd_ref, o_ref,
                idx_smem, idx_sem, gbuf, gsem, xb_scr, h_scr, *, tmb, rc, nf, nn, tf, tn):
    b = pl.program_id(0)
    t = pl.program_id(1)
    n_used = nu_ref[0]
    active = b < n_used
    slot = b % 2
    nxt = 1 - slot
    per = tmb // nf
    win = 2 * tmb

    def idx_copy(s, sl):
        return pltpu.make_async_copy(idx_hbm.at[pl.ds(pl.multiple_of(s * win, win), win)],
                                     idx_smem.at[pl.ds(pl.multiple_of(sl * win, win), win)],
                                     idx_sem.at[sl])

    def gather_next_slice():
        base = nxt * win + sh_ref[b + 1] + t * per
        for i in range(per):
            src_row = pl.multiple_of(idx_smem[base + i], ROW_TILE)
            dst_row = pl.multiple_of((t * per + i) * ROW_TILE, ROW_TILE)
            pltpu.make_async_copy(x_hbm.at[pl.ds(src_row, ROW_TILE)], gbuf.at[nxt, pl.ds(dst_row, ROW_TILE)],
                                  gsem.at[nxt]).start()

    @pl.when((b <= n_used) & (t == 0))
    def _():
        @pl.when(b == 0)
        def _():
            c = idx_copy(0, 0)
            c.start()
            c.wait()
            _gather_rows(idx_smem, sh_ref[0], x_hbm, gbuf.at[0], gsem.at[0], tmb)
            idx_copy(1, 1).start()

        _wait_rows(x_hbm, gbuf.at[slot], gsem.at[slot], tmb)

    @pl.when(active & (t == 0))
    def _():
        idx_copy(b + 1, nxt).wait()

        @pl.when(b + 2 <= n_used)
        def _():
            idx_copy(b + 2, slot).start()

        def body(c, carry):
            r0 = pl.multiple_of(c * rc, rc)
            r = pl.ds(r0, rc)
            for s in range(ROW_TILE):
                c_lo, c_hi = _packed_chunks(s, D_MODEL)
                lo, hi = _unpack_rows(_load_row_tile_words(gbuf.at[slot], r0, rc, s))
                xb_scr[r, c_lo * LANES:(c_lo + 1) * LANES] = lo.astype(BF16)
                xb_scr[r, c_hi * LANES:(c_hi + 1) * LANES] = hi.astype(BF16)
            return carry
        lax.fori_loop(0, tmb // rc, body, 0)

    @pl.when(active & (t < nf))
    def _():
        gather_next_slice()
        x = xb_scr[...]
        g = jnp.dot(x, wg_ref[0], preferred_element_type=F32) + bg_ref[0]
        u = jnp.dot(x, wu_ref[0], preferred_element_type=F32) + bu_ref[0]
        g = jnp.minimum(g, SWIGLU_LIMIT)
        u = jnp.clip(u, -SWIGLU_LIMIT, SWIGLU_LIMIT)
        h = (g * _sigmoid(SWIGLU_ALPHA * g) * (u + 1.0)).astype(BF16)
        for f in range(nf):
            @pl.when(t == f)
            def _():
                h_scr[:, f * tf:(f + 1) * tf] = h

    @pl.when(active & (t >= nf))
    def _():
        y = jnp.dot(h_scr[...], wd_ref[0], preferred_element_type=F32) + bd_ref[0]
        sub_per_tile = tn // (2 * LANES)
        s0 = (t - nf) * sub_per_tile
        for c in range(tmb // rc):
            yc = y[c * rc:(c + 1) * rc]
            for q in range(sub_per_tile):
                lo = yc[:, q * LANES:(q + 1) * LANES]
                hi = yc[:, tn // 2 + q * LANES:tn // 2 + (q + 1) * LANES]
                o_ref[pl.ds(c * rc * ROW_TILE + s0 + q, rc, stride=ROW_TILE), :] = _pack_words(lo, hi)

    @pl.when(jnp.logical_not(active) & (t == nf + nn - 1))
    def _():
        o_ref[...] = jnp.zeros(o_ref.shape, U32)


def _moe_experts(block_e, n_used, shift, row_win, x_packed, wg, bg, wu, bu, wd, bd, tmb):
    nblk = block_e.shape[0]
    tf = MOE_TILE
    tn = MOE_TILE_DOWN
    nf = D_FF // tf
    nn = D_MODEL // tn
    rc = min(256, tmb)

    def fsel(b, t, nu):
        return jnp.where(b < nu[0], jnp.minimum(t, nf - 1), nf - 1)

    def nsel(b, t, nu):
        return jnp.where(b < nu[0], jnp.maximum(t - nf, 0), nn - 1)

    grid_spec = pltpu.PrefetchScalarGridSpec(
        num_scalar_prefetch=3,
        grid=(nblk, nf + nn),
        in_specs=[
            pl.BlockSpec(memory_space=pl.ANY),
            pl.BlockSpec(memory_space=pl.ANY),
            pl.BlockSpec((1, D_MODEL, tf), lambda b, t, be, nu, sh: (be[b], 0, fsel(b, t, nu))),
            pl.BlockSpec((1, 1, tf), lambda b, t, be, nu, sh: (be[b], 0, fsel(b, t, nu))),
            pl.BlockSpec((1, D_MODEL, tf), lambda b, t, be, nu, sh: (be[b], 0, fsel(b, t, nu))),
            pl.BlockSpec((1, 1, tf), lambda b, t, be, nu, sh: (be[b], 0, fsel(b, t, nu))),
            pl.BlockSpec((1, D_FF, tn), lambda b, t, be, nu, sh: (be[b], 0, nsel(b, t, nu))),
            pl.BlockSpec((1, 1, tn), lambda b, t, be, nu, sh: (be[b], 0, nsel(b, t, nu))),
        ],
        out_specs=pl.BlockSpec((tmb * ROW_TILE, LANES), lambda b, t, be, nu, sh: (b, 0)),
        scratch_shapes=[
            pltpu.SMEM((4 * tmb,), I32),
            pltpu.SemaphoreType.DMA((2,)),
            pltpu.VMEM((2, tmb * ROW_TILE, LANES), U32),
            pltpu.SemaphoreType.DMA((2,)),
            pltpu.VMEM((tmb, D_MODEL), BF16),
            pltpu.VMEM((tmb, D_FF), BF16),
        ],
    )
    return pl.pallas_call(
        functools.partial(_moe_kernel, tmb=tmb, rc=rc, nf=nf, nn=nn, tf=tf, tn=tn),
        grid_spec=grid_spec,
        out_shape=jax.ShapeDtypeStruct((nblk * tmb * ROW_TILE, LANES), U32),
        compiler_params=_cparams(("arbitrary", "arbitrary"), disable_bounds_checks=True),
        name="moe_experts",
    )(block_e, n_used, shift, row_win, x_packed, wg, bg, wu, bu, wd, bd)


def _combine_kernel(idx_hbm, y_hbm, x1_ref, gate_ref, g_ref, b_ref, o_ref,
                    idx_smem, idx_sem, gbuf0, gbuf1, gsem, *, tc):
    i = pl.program_id(0)
    last = pl.num_programs(0) - 1
    count = TOP_K * tc
    gbufs = (gbuf0, gbuf1)

    def idx_copy(s, sl):
        return pltpu.make_async_copy(idx_hbm.at[pl.ds(pl.multiple_of(s * count, count), count)],
                                     idx_smem.at[pl.ds(pl.multiple_of(sl * count, count), count)],
                                     idx_sem.at[sl])

    @pl.when(i == 0)
    def _():
        c = idx_copy(0, 0)
        c.start()
        c.wait()
        _gather_rows(idx_smem, 0, y_hbm, gbuf0, gsem.at[0], count)
        idx_copy(jnp.minimum(1, last), 1).start()

    for parity in range(2):
        @pl.when(i % 2 == parity)
        def _():
            _combine_step(parity, i, last, count, tc, idx_copy, idx_smem, y_hbm, gbufs, gsem,
                          x1_ref, gate_ref, g_ref, b_ref, o_ref)


def _combine_step(slot, i, last, count, tc, idx_copy, idx_smem, y_hbm, gbufs, gsem,
                  x1_ref, gate_ref, g_ref, b_ref, o_ref):
    nslot = 1 - slot
    _wait_rows(y_hbm, gbufs[slot], gsem.at[slot], count)
    idx_copy(jnp.minimum(i + 1, last), nslot).wait()
    for r in range(count):
        src_row = pl.multiple_of(idx_smem[nslot * count + r], ROW_TILE)
        pltpu.make_async_copy(y_hbm.at[pl.ds(src_row, ROW_TILE)],
                              gbufs[nslot].at[pl.ds(r * ROW_TILE, ROW_TILE)], gsem.at[nslot]).start(priority=r % 2)
    idx_copy(jnp.minimum(i + 2, last), slot).start()

    gates = gate_ref[...]
    gk = [gates[:, k:k + 1] for k in range(TOP_K)]
    ssum = jnp.zeros((tc, 1), F32)
    ssq = jnp.zeros((tc, 1), F32)
    for s in range(ROW_TILE):
        chunks = _packed_chunks(s, MOE_TILE_DOWN)
        z = [DN_ALPHA * x1_ref[:, c * LANES:(c + 1) * LANES] for c in chunks]
        for k in range(TOP_K):
            halves = _unpack_rows(_load_row_tile_words(gbufs[slot], k * tc, tc, s))
            z = [zc + gk[k] * h for zc, h in zip(z, halves)]
        for c, zc in zip(chunks, z):
            o_ref[:, c * LANES:(c + 1) * LANES] = zc
            ssum = ssum + jnp.sum(zc, axis=-1, keepdims=True)
            ssq = ssq + jnp.sum(zc * zc, axis=-1, keepdims=True)
    mu = ssum / D_MODEL
    inv = lax.rsqrt(ssq / D_MODEL - mu * mu + LN_EPS)
    for c in range(D_MODEL // LANES):
        cols = slice(c * LANES, (c + 1) * LANES)
        o_ref[:, cols] = (o_ref[:, cols] - mu) * inv * g_ref[:, cols] + b_ref[:, cols]

    @pl.when(i == last)
    def _():
        _wait_rows(y_hbm, gbufs[nslot], gsem.at[nslot], count)
        idx_copy(last, slot).wait()


def _combine(dest_tiles, yb, x1, gates, g2, b2, tc):
    n = x1.shape[0]
    const = lambda i: (0, 0)
    row = lambda i: (i, 0)
    return pl.pallas_call(
        functools.partial(_combine_kernel, tc=tc),
        grid=(n // tc,),
        in_specs=[
            pl.BlockSpec(memory_space=pl.ANY),
            pl.BlockSpec(memory_space=pl.ANY),
            pl.BlockSpec((tc, D_MODEL), row),
            pl.BlockSpec((tc, TOP_K), row),
            pl.BlockSpec((1, D_MODEL), const),
            pl.BlockSpec((1, D_MODEL), const),
        ],
        out_specs=pl.BlockSpec((tc, D_MODEL), row),
        out_shape=jax.ShapeDtypeStruct((n, D_MODEL), F32),
        scratch_shapes=[
            pltpu.SMEM((2 * TOP_K * tc,), I32),
            pltpu.SemaphoreType.DMA((2,)),
            pltpu.VMEM((TOP_K * tc * ROW_TILE, LANES), U32),
            pltpu.VMEM((TOP_K * tc * ROW_TILE, LANES), U32),
            pltpu.SemaphoreType.DMA((2,)),
        ],
        compiler_params=_cparams(("arbitrary",), disable_bounds_checks=True),
        name="moe_combine_ln",
    )(dest_tiles, yb, x1, gates, g2, b2)


def _route_meta(top_e, tmb):
    nt = top_e.shape[0]
    nblk = (nt * TOP_K) // tmb + N_EXPERTS
    e_ids = jnp.arange(N_EXPERTS, dtype=I32)
    sel = top_e[:, :, None] == e_ids[None, None, :]
    onehot = sel.sum(axis=1).astype(I32)
    incl = jnp.cumsum(onehot, axis=0)
    counts = incl[-1]
    padded = (counts + tmb - 1) // tmb * tmb
    pend = jnp.cumsum(padded)
    pstart = pend - padded
    start = jnp.cumsum(counts) - counts
    dest = jnp.where(sel, (pstart[None, :] + incl - onehot)[:, None, :], 0).sum(axis=-1).astype(I32)
    n_used = (pend[-1] // tmb).astype(I32)
    blk = jnp.arange(nblk, dtype=I32)
    be = jnp.clip(jnp.searchsorted(pend, blk * tmb, side='right'), 0, N_EXPERTS - 1).astype(I32)
    be = jnp.where(blk < n_used, be, be[jnp.maximum(n_used - 1, 0)])
    order = jnp.argsort(top_e.reshape(-1), stable=True).astype(I32)
    lines = (nt * TOP_K) // LANES + 2 * tmb // LANES + 1
    tok_sorted = jnp.zeros((lines * LANES,), I32).at[:nt * TOP_K].set((order // TOP_K) * ROW_TILE)
    win = jnp.clip(blk * tmb - pstart[be] + start[be], 0, nt * TOP_K)
    line_idx = (win // LANES)[:, None] + jnp.arange(2 * tmb // LANES, dtype=I32)[None, :]
    row_win = jnp.take(tok_sorted.reshape(lines, LANES), line_idx, axis=0).reshape(-1)
    shift = (win % LANES).astype(I32)
    return be, n_used.reshape(1), shift, row_win, dest * ROW_TILE


def kernel(x_prompt, x_sample, ln_in_g, ln_in_b, w_in, lam_q1, lam_k1, lam_q2, lam_k2, subln_g, w_attn_proj, ssm_lam_re, ssm_lam_im, ssm_log_dt, ssm_b_re, ssm_b_im, ssm_c_re, ssm_c_im, ssm_d, w_glu, b_glu, w_ssm_proj, w_out, ln1_g, ln1_b, w_router, b_router, w_gate, b_gate, w_up, b_up, w_down, b_down, ln2_g, ln2_b):
    l = 0
    row = lambda v: v.reshape(1, -1).astype(F32)
    lambda_init = 0.8 - 0.6 * math.exp(-0.3 * l)
    lam = (jnp.exp(jnp.sum(lam_q1[l].astype(F32) * lam_k1[l].astype(F32)))
           - jnp.exp(jnp.sum(lam_q2[l].astype(F32) * lam_k2[l].astype(F32))) + lambda_init).reshape(1, 1)
    w_in_b = w_in[l].astype(BF16)
    wglu_b = w_glu[l].astype(BF16)
    wap_b = w_attn_proj[l].astype(BF16)
    wsp_b = w_ssm_proj[l].astype(BF16)
    wout_b = w_out[l].astype(BF16)
    wg_b = w_gate[l].astype(BF16)
    wu_b = w_up[l].astype(BF16)
    wd_b = w_down[l].astype(BF16)
    wr = w_router[l].astype(F32)
    wr_hi = wr.astype(BF16)
    wr_split = jnp.concatenate([wr_hi, (wr - wr_hi.astype(F32)).astype(BF16)], axis=1)
    s5_ops = _s5_operators(ssm_lam_re[l], ssm_lam_im[l], ssm_log_dt[l], ssm_b_re[l], ssm_b_im[l],
                           ssm_c_re[l], ssm_c_im[l])
    d_tiled = jnp.tile(ssm_d[l].astype(F32).reshape(SSM_GROUPS, 1, SSM_GROUP), (1, 1, S5_CHUNK))

    def pre_moe(x3):
        batch, seq, _ = x3.shape
        x = x3.reshape(batch * seq, D_MODEL)
        proj = _inproj(x, row(ln_in_g), row(ln_in_b), w_in_b)
        cos_t, sin_t = _rope_tables(seq)
        attn = _attention(proj, lam, cos_t, sin_t, row(subln_g[l]), batch, seq, lambda_init)
        yact = _s5_branch(proj, s5_ops, d_tiled, batch, seq)
        merged = _merge(yact, attn, proj, wglu_b, row(b_glu[l]), wap_b, wsp_b)
        return _outproj(x, merged, wout_b, row(ln_in_g), row(ln_in_b), row(ln1_g[l]), row(ln1_b[l]),
                        wr_split, row(b_router[l]))

    parts = [pre_moe(x_prompt), pre_moe(x_sample)]
    xp_all = jnp.concatenate([p[1] for p in parts], axis=0)
    te_all = jnp.concatenate([p[2] for p in parts], axis=0)
    nt = xp_all.shape[0]
    tmb = 1024 if nt >= 8192 else 128
    be, n_used, shift, row_win, dest = _route_meta(te_all, tmb)
    yb = _moe_experts(be, n_used, shift, row_win, xp_all, wg_b, b_gate[l].astype(F32)[:, None, :],
                      wu_b, b_up[l].astype(F32)[:, None, :], wd_b, b_down[l].astype(F32)[:, None, :], tmb)

    outs = []
    off = 0
    for x3, p in zip((x_prompt, x_sample), parts):
        n = p[0].shape[0]
        tc = min(256, n)
        d = dest[off:off + n].reshape(n // tc, tc, TOP_K)
        d = jnp.transpose(d, (0, 2, 1)).reshape(-1)
        out = _combine(d, yb, p[0], p[3], row(ln2_g[l]), row(ln2_b[l]), tc)
        outs.append(out.reshape(x3.shape))
        off += n
    return tuple(outs)
```
